```python
import jax, jax.numpy as jnp
from jax import lax
import numpy as np

D_MODEL = 2048
BATCH = 2
SEQ = 4096
DEPTH = 1

N_META = 16
HEAD_DIM = 128
FOX_HEADS = 8
SB_HEADS = 8
FOX_WIDTH = FOX_HEADS * HEAD_DIM
SB_WIDTH = SB_HEADS * HEAD_DIM
D_FF = 5632
Q_BLOCK = 128
RMS_EPS = 1e-6
FFN_RESIDUAL_WEIGHT = 0.5
FORGET_BIAS_INIT = 3.0

IN_SIZES = [FOX_WIDTH, FOX_WIDTH, FOX_WIDTH, FOX_HEADS,
            SB_WIDTH, SB_WIDTH, SB_WIDTH,
            D_MODEL, D_MODEL]
IN_PROJ_WIDTH = sum(IN_SIZES)
IN_SPLIT_POINTS = [int(v) for v in np.cumsum(IN_SIZES)[:-1]]

kernel_name = "hybrid_fox_stickbreak_macaron"


def _rmsnorm(x, gain):
    x32 = x.astype(jnp.float32)
    y = x32 * lax.rsqrt(jnp.mean(x32 * x32, axis=-1, keepdims=True) + RMS_EPS)
    return (y * gain.astype(jnp.float32)).astype(x.dtype)


def _swiglu(x, w_gate, w_up, w_down):
    return (jax.nn.silu(x @ w_gate) * (x @ w_up)) @ w_down


def _query_blocks(total_len):
    blocks = [(0, N_META)]
    start = N_META
    while start < total_len:
        end = min(start + Q_BLOCK, total_len)
        blocks.append((start, end))
        start = end
    return blocks


def _fox_attention(q, k, v, log_f_cum):
    L = q.shape[1]
    scale = HEAD_DIM ** -0.5
    outs = []
    for q0, q1 in _query_blocks(L):
        qb, kb, vb = q[:, q0:q1], k[:, :q1], v[:, :q1]
        logits = jnp.einsum('bqhd,bkhd->bhqk', qb, kb).astype(jnp.float32) * scale
        decay = log_f_cum[:, :, q0:q1, None] - log_f_cum[:, :, None, :q1]
        t = jnp.arange(q0, q1)[:, None]
        s = jnp.arange(q1)[None, :]
        logits = jnp.where(s <= t, logits + decay, -jnp.inf)
        p = jax.nn.softmax(logits, axis=-1)
        outs.append(jnp.einsum('bhqk,bkhd->bqhd', p.astype(vb.dtype), vb))
    return jnp.concatenate(outs, axis=1)


def _stick_breaking_attention(q, k, v):
    L = q.shape[1]
    scale = HEAD_DIM ** -0.5
    outs = []
    for q0, q1 in _query_blocks(L):
        qb, kb, vb = q[:, q0:q1], k[:, :q1], v[:, :q1]
        z = jnp.einsum('bqhd,bkhd->bhqk', qb, kb).astype(jnp.float32) * scale
        t = jnp.arange(q0, q1)[:, None]
        s = jnp.arange(q1)[None, :]
        strict = s < t
        log_beta = jax.nn.log_sigmoid(z)
        log_one_minus = jnp.where(strict, log_beta - z, 0.0)
        later = lax.cumsum(log_one_minus, axis=3, reverse=True) - log_one_minus
        w = jnp.where(strict, jnp.exp(log_beta + later), 0.0)
        outs.append(jnp.einsum('bhqk,bkhd->bqhd', w.astype(vb.dtype), vb))
    return jnp.concatenate(outs, axis=1)


def _hybrid_mixer(xn, w_in, b_forget, fox_q_norm, fox_k_norm, w_branch_fox, w_branch_sb, w_out):
    B, L, _ = xn.shape
    proj = xn @ w_in
    fq, fk, fv, f_logit, sq, sk, sv, g_fox, g_sb = jnp.split(proj, IN_SPLIT_POINTS, axis=-1)
    heads = lambda a, n: a.reshape(B, L, n, HEAD_DIM)
    fq = _rmsnorm(heads(fq, FOX_HEADS), fox_q_norm)
    fk = _rmsnorm(heads(fk, FOX_HEADS), fox_k_norm)
    log_f = jax.nn.log_sigmoid((f_logit + b_forget).astype(jnp.float32))
    log_f_cum = jnp.transpose(lax.cumsum(log_f, axis=1), (0, 2, 1))
    o_fox = _fox_attention(fq, fk, heads(fv, FOX_HEADS), log_f_cum).reshape(B, L, FOX_WIDTH)
    o_sb = _stick_breaking_attention(heads(sq, SB_HEADS), heads(sk, SB_HEADS),
                                     heads(sv, SB_HEADS)).reshape(B, L, SB_WIDTH)
    merged = jax.nn.sigmoid(g_fox) * (o_fox @ w_branch_fox) + jax.nn.sigmoid(g_sb) * (o_sb @ w_branch_sb)
    return merged @ w_out


def setup_inputs(seed: int = 0) -> dict:
    key = jax.random.key(seed)
    ks = jax.random.split(key, 20)
    f32 = jnp.float32
    nrm = lambda k, shape, fan_in: jax.random.normal(k, shape, f32) * (fan_in ** -0.5)
    gain = lambda k, shape: 1.0 + 0.02 * jax.random.normal(k, shape, f32)
    return {
        "x": jax.random.normal(ks[0], (BATCH, SEQ, D_MODEL), f32),
        "meta_tokens": jax.random.normal(ks[1], (N_META, D_MODEL), f32),
        "ffn1_norm": gain(ks[2], (DEPTH, D_MODEL)),
        "ffn1_w_gate": nrm(ks[3], (DEPTH, D_MODEL, D_FF), D_MODEL),
        "ffn1_w_up": nrm(ks[4], (DEPTH, D_MODEL, D_FF), D_MODEL),
        "ffn1_w_down": nrm(ks[5], (DEPTH, D_FF, D_MODEL), D_FF),
        "mix_norm": gain(ks[6], (DEPTH, D_MODEL)),
        "w_in": nrm(ks[7], (DEPTH, D_MODEL, IN_PROJ_WIDTH), D_MODEL),
        "b_forget": FORGET_BIAS_INIT + 0.1 * jax.random.normal(ks[8], (DEPTH, FOX_HEADS), f32),
        "fox_q_norm": gain(ks[9], (DEPTH, FOX_HEADS, HEAD_DIM)),
        "fox_k_norm": gain(ks[10], (DEPTH, FOX_HEADS, HEAD_DIM)),
        "w_branch_fox": nrm(ks[11], (DEPTH, FOX_WIDTH, D_MODEL), FOX_WIDTH),
        "w_branch_sb": nrm(ks[12], (DEPTH, SB_WIDTH, D_MODEL), SB_WIDTH),
        "w_out": nrm(ks[13], (DEPTH, D_MODEL, D_MODEL), D_MODEL),
        "ffn2_norm": gain(ks[14], (DEPTH, D_MODEL)),
        "ffn2_w_gate": nrm(ks[15], (DEPTH, D_MODEL, D_FF), D_MODEL),
        "ffn2_w_up": nrm(ks[16], (DEPTH, D_MODEL, D_FF), D_MODEL),
        "ffn2_w_down": nrm(ks[17], (DEPTH, D_FF, D_MODEL), D_FF),
    }


def reference(x, meta_tokens, ffn1_norm, ffn1_w_gate, ffn1_w_up, ffn1_w_down, mix_norm, w_in, b_forget,
              fox_q_norm, fox_k_norm, w_branch_fox, w_branch_sb, w_out, ffn2_norm, ffn2_w_gate, ffn2_w_up,
              ffn2_w_down):
    B = x.shape[0]
    meta = jnp.broadcast_to(meta_tokens[None].astype(x.dtype), (B, N_META, D_MODEL))
    h = jnp.concatenate([meta, x], axis=1)
    for layer in range(DEPTH):
        h = h + FFN_RESIDUAL_WEIGHT * _swiglu(_rmsnorm(h, ffn1_norm[layer]), ffn1_w_gate[layer],
                                              ffn1_w_up[layer], ffn1_w_down[layer])
        h = h + _hybrid_mixer(_rmsnorm(h, mix_norm[layer]), w_in[layer], b_forget[layer],
                              fox_q_norm[layer], fox_k_norm[layer], w_branch_fox[layer],
                              w_branch_sb[layer], w_out[layer])
        h = h + FFN_RESIDUAL_WEIGHT * _swiglu(_rmsnorm(h, ffn2_norm[layer]), ffn2_w_gate[layer],
                                              ffn2_w_up[layer], ffn2_w_down[layer])
    return h[:, N_META:]
```

```python
import functools

import jax
import jax.numpy as jnp
from jax import lax
from jax.experimental import pallas as pl
from jax.experimental.pallas import tpu as pltpu

F32 = jnp.float32
BF16 = jnp.bfloat16

D_MODEL = 2048
D_FF = 5632
N_META = 16
HEAD_DIM = 128
N_HEADS = 8
WIDTH = N_HEADS * HEAD_DIM
RMS_EPS = 1e-6
FFN_RESIDUAL_WEIGHT = 0.5
SCALE = HEAD_DIM ** -0.5

LANES = 128
V7X_VMEM_BYTES = 64 * 2**20
VMEM_LIMIT = 58 * 2**20

ROW_TILE_ALL = 912
ROW_TILE_REAL = 1024
FFN_TILE_F = 256
PROJ_TILE_N = 1024
OUT_TILE_N = 256
ATT_TQ = 512
FOX_TK = 512
SB_TK = 256
DECAY_TERMS = 3


def _mm(a, b):
    return jnp.dot(a, b, preferred_element_type=F32)


def _mm_nt(a, b):
    return lax.dot_general(a, b, (((1,), (1,)), ((), ())), preferred_element_type=F32)


def _split_bf16(x, n):
    parts = []
    r = x
    for _ in range(n):
        p = r.astype(BF16)
        parts.append(p)
        r = r - p.astype(F32)
    return parts


def _rmsnorm_rows(h, gain):
    ms = jnp.mean(h * h, axis=-1, keepdims=True)
    return h * lax.rsqrt(ms + RMS_EPS) * gain


def _log_sigmoid(z):
    return jnp.minimum(z, 0.0) - jnp.log1p(jnp.exp(-jnp.abs(z)))


def _params(n_grid_axes):
    return pltpu.CompilerParams(dimension_semantics=("arbitrary",) * n_grid_axes,
                                vmem_limit_bytes=VMEM_LIMIT)


def _ffn_kernel(*refs, meta_row, emit_norm):
    refs = list(refs)
    x_ref = refs.pop(0)
    meta_ref = refs.pop(0) if meta_row is not None else None
    gain_ref, wg_ref, wu_ref, wd_ref = refs[:4]
    refs = refs[4:]
    if emit_norm:
        ngain_ref, o_ref, n_ref, xn_ref = refs
    else:
        o_ref, xn_ref = refs
    i = pl.program_id(0)
    j = pl.program_id(1)

    @pl.when(j == 0)
    def _():
        if meta_row is None:
            o_ref[...] = x_ref[...]
        else:
            last = pl.num_programs(0) - 1

            @pl.when(i < last)
            def _():
                o_ref[...] = x_ref[...]

            @pl.when(i == last)
            def _():
                o_ref[0:meta_row, :] = x_ref[0:meta_row, :]
                o_ref[meta_row:meta_row + N_META, :] = meta_ref[...]
        xn_ref[...] = _rmsnorm_rows(o_ref[...], gain_ref[...]).astype(BF16)

    xn = xn_ref[...]
    g = _mm(xn, wg_ref[...].astype(BF16))
    u = _mm(xn, wu_ref[...].astype(BF16))
    a = (g * jax.nn.sigmoid(g)) * (u * FFN_RESIDUAL_WEIGHT)
    o_ref[...] += _mm(a.astype(BF16), wd_ref[...].astype(BF16))

    if emit_norm:
        @pl.when(j == pl.num_programs(1) - 1)
        def _():
            n_ref[...] = _rmsnorm_rows(o_ref[...], ngain_ref[...]).astype(BF16)


def _ffn(x, meta, gain, wg, wu, wd, *, rows, tm, next_gain=None):
    ni = rows // tm
    nj = D_FF // FFN_TILE_F
    emit_norm = next_gain is not None
    in_specs = [pl.BlockSpec((tm, D_MODEL), lambda i, j: (i, 0))]
    args = [x]
    meta_row = None
    if meta is not None:
        meta_row = x.shape[0] - (ni - 1) * tm
        assert meta_row + N_META == tm
        in_specs.append(pl.BlockSpec((N_META, D_MODEL), lambda i, j: (0, 0)))
        args.append(meta)
    in_specs += [
        pl.BlockSpec((1, D_MODEL), lambda i, j: (0, 0)),
        pl.BlockSpec((D_MODEL, FFN_TILE_F), lambda i, j: (0, j)),
        pl.BlockSpec((D_MODEL, FFN_TILE_F), lambda i, j: (0, j)),
        pl.BlockSpec((FFN_TILE_F, D_MODEL), lambda i, j: (j, 0)),
    ]
    args += [gain.reshape(1, D_MODEL), wg, wu, wd]
    out_shape = [jax.ShapeDtypeStruct((rows, D_MODEL), F32)]
    out_specs = [pl.BlockSpec((tm, D_MODEL), lambda i, j: (i, 0))]
    if emit_norm:
        in_specs.append(pl.BlockSpec((1, D_MODEL), lambda i, j: (0, 0)))
        args.append(next_gain.reshape(1, D_MODEL))
        out_shape.append(jax.ShapeDtypeStruct((rows, D_MODEL), BF16))
        out_specs.append(pl.BlockSpec((tm, D_MODEL), lambda i, j: (i, 0)))
    res = pl.pallas_call(
        functools.partial(_ffn_kernel, meta_row=meta_row, emit_norm=emit_norm),
        grid=(ni, nj),
        in_specs=in_specs,
        out_specs=out_specs,
        out_shape=out_shape,
        scratch_shapes=[pltpu.VMEM((tm, D_MODEL), BF16)],
        compiler_params=_params(2),
        name="ffn_norm" if emit_norm else "ffn",
    )(*args)
    return res if emit_norm else res[0]


def _proj_kernel(n_ref, w_ref, *refs, epilogue):
    p_ref = refs[0] if len(refs) == 3 else None
    o_ref, wb_ref = refs[-2:]

    @pl.when(pl.program_id(1) == 0)
    def _():
        wb_ref[...] = w_ref[...].astype(BF16)

    y = _mm(n_ref[...], wb_ref[...])
    epilogue(y, p_ref, o_ref, pl.program_id(0))


def _fox_epilogue(y, p_ref, o_ref, j):
    @pl.when(j < 2)
    def _():
        for h in range(y.shape[1] // HEAD_DIM):
            sl = slice(h * HEAD_DIM, (h + 1) * HEAD_DIM)
            o_ref[:, sl] = _rmsnorm_rows(y[:, sl], p_ref[:, sl]).astype(o_ref.dtype)

    @pl.when(j >= 2)
    def _():
        o_ref[...] = y.astype(o_ref.dtype)


def _scale_epilogue(y, p_ref, o_ref, j):
    o_ref[...] = (y * p_ref[...]).astype(o_ref.dtype)


def _gate_epilogue(y, p_ref, o_ref, j):
    o_ref[...] = jax.nn.sigmoid(y).astype(o_ref.dtype)


def _logf_epilogue(y, p_ref, o_ref, j):
    o_ref[...] = _log_sigmoid(y + p_ref[...]).astype(o_ref.dtype)


def _proj(n, w, col0, n_cols, p, *, rows, tm, tn, epilogue, out_dtype, name):
    assert col0 % tn == 0 and n_cols % tn == 0
    nj, ni, j0 = n_cols // tn, rows // tm, col0 // tn
    in_specs = [
        pl.BlockSpec((tm, D_MODEL), lambda j, i: (i, 0)),
        pl.BlockSpec((D_MODEL, tn), lambda j, i: (0, j0 + j)),
    ]
    args = [n, w]
    if p is not None:
        in_specs.append(pl.BlockSpec((1, tn), lambda j, i: (0, j)))
        args.append(p)
    return pl.pallas_call(
        functools.partial(_proj_kernel, epilogue=epilogue),
        grid=(nj, ni),
        in_specs=in_specs,
        out_specs=pl.BlockSpec((tm, tn), lambda j, i: (i, j)),
        out_shape=jax.ShapeDtypeStruct((rows, n_cols), out_dtype),
        scratch_shapes=[pltpu.VMEM((D_MODEL, tn), BF16)],
        compiler_params=_params(2),
        name=name,
    )(*args)


def _decay_kernel(lf_ref, qc_ref, kc_ref, c_ref, *, n_batch, seq):
    T = DECAY_TERMS
    n_real = n_batch * seq
    r_i = lax.broadcasted_iota(jnp.int32, (LANES, LANES), 0)
    c_i = lax.broadcasted_iota(jnp.int32, (LANES, LANES), 1)
    tri = jnp.where(r_i >= c_i, 1.0, 0.0).astype(BF16)

    def cum_block(x, carry):
        acc = carry
        for p in _split_bf16(x, T):
            acc = acc + _mm(tri, p)
        return acc

    xm = jnp.concatenate([lf_ref[n_real:n_real + N_META, :], jnp.zeros((LANES - N_META, LANES), F32)], axis=0)
    cm = cum_block(xm, jnp.zeros((1, LANES), F32))
    c_ref[n_real:n_real + N_META, :] = cm[:N_META, :]
    base = cm[N_META - 1:N_META, :]

    for b in range(n_batch):
        def body(t, carry, b=b):
            r0 = pl.multiple_of(b * seq + t * LANES, LANES)
            c = cum_block(lf_ref[pl.ds(r0, LANES), :], carry)
            c_ref[pl.ds(r0, LANES), :] = c
            return c[LANES - 1:LANES, :]
        lax.fori_loop(0, seq // LANES, body, base)

    k_i = lax.broadcasted_iota(jnp.int32, (T * LANES, LANES), 0)
    n_i = lax.broadcasted_iota(jnp.int32, (T * LANES, LANES), 1)
    sel_q = jnp.zeros((T * LANES, LANES), F32)
    sel_k = jnp.zeros((T * LANES, LANES), F32)
    lane = lax.broadcasted_iota(jnp.int32, (1, LANES), 1)
    ones_q = jnp.zeros((1, LANES), F32)
    ones_k = jnp.zeros((1, LANES), F32)
    for h in range(N_HEADS):
        for p in range(T):
            row = p * LANES + h
            sel_q = jnp.where((k_i == row) & (n_i == 2 * T * h + p), 1.0, sel_q)
            sel_k = jnp.where((k_i == row) & (n_i == 2 * T * h + T + p), -1.0, sel_k)
            ones_q = jnp.where(lane == 2 * T * h + T + p, 1.0, ones_q)
            ones_k = jnp.where(lane == 2 * T * h + p, 1.0, ones_k)
    sel_q = sel_q.astype(BF16)
    sel_k = sel_k.astype(BF16)

    def expand(c):
        pieces = jnp.concatenate(_split_bf16(c, T), axis=1)
        return ((_mm(pieces, sel_q) + ones_q).astype(BF16), (_mm(pieces, sel_k) + ones_k).astype(BF16))

    chunk = 1024
    def ebody(t, _):
        r0 = pl.multiple_of(t * chunk, chunk)
        q, k = expand(c_ref[pl.ds(r0, chunk), :])
        qc_ref[pl.ds(r0, chunk), :] = q
        kc_ref[pl.ds(r0, chunk), :] = k
        return 0
    lax.fori_loop(0, n_real // chunk, ebody, 0)
    _, km = expand(c_ref[n_real:n_real + N_META, :])
    kc_ref[n_real:n_real + N_META, :] = km


def _decay(lf, *, n_batch, seq):
    n_real = n_batch * seq
    return pl.pallas_call(
        functools.partial(_decay_kernel, n_batch=n_batch, seq=seq),
        out_shape=[jax.ShapeDtypeStruct((n_real, LANES), BF16),
                   jax.ShapeDtypeStruct((n_real + N_META, LANES), BF16)],
        scratch_shapes=[pltpu.VMEM((n_real + N_META, LANES), F32)],
        compiler_params=pltpu.CompilerParams(vmem_limit_bytes=VMEM_LIMIT),
        name="decay",
    )(lf)


def _load_meta(dst_ref, src_ref, col0=0):
    dst_ref[0:N_META, col0:col0 + src_ref.shape[1]] = src_ref[...]


def _fox_kernel(q_ref, qc_ref, k_ref, kc_ref, v_ref, mk_ref, mkc_ref, mv_ref, o_ref, kpad_ref, vpad_ref,
                *, seq):
    T = DECAY_TERMS
    h = pl.program_id(1)
    kpad_ref[...] = jnp.zeros_like(kpad_ref)
    vpad_ref[...] = jnp.zeros_like(vpad_ref)
    _load_meta(kpad_ref, mk_ref)
    _load_meta(kpad_ref, mkc_ref, HEAD_DIM)
    _load_meta(vpad_ref, mv_ref)

    lane = lax.broadcasted_iota(jnp.int32, (1, LANES), 1)
    head_lanes = (lane >= 2 * T * h) & (lane < 2 * T * (h + 1))
    meta_valid = lax.broadcasted_iota(jnp.int32, (ATT_TQ, LANES), 1) < N_META
    causal = (lax.broadcasted_iota(jnp.int32, (ATT_TQ, FOX_TK), 1)
              <= lax.broadcasted_iota(jnp.int32, (ATT_TQ, FOX_TK), 0))

    def online(s, vblk, m, l, acc):
        m_new = jnp.maximum(m, jnp.max(s, axis=1, keepdims=True))
        alpha = jnp.exp(m - m_new)
        p = jnp.exp(s - m_new)
        l = alpha * l + jnp.sum(p, axis=1, keepdims=True)
        acc = alpha * acc + _mm(p.astype(BF16), vblk)
        return m_new, l, acc

    def q_block(i, _):
        r0 = pl.multiple_of(i * ATT_TQ, ATT_TQ)
        qx = jnp.where(head_lanes, qc_ref[pl.ds(r0, ATT_TQ), :], jnp.zeros((), BF16))
        qa = jnp.concatenate([q_ref[pl.ds(r0, ATT_TQ), :], qx], axis=1)

        def keys(c0):
            return jnp.concatenate([k_ref[pl.ds(c0, FOX_TK), :], kc_ref[pl.ds(c0, FOX_TK), :]], axis=1)

        s = jnp.where(meta_valid, _mm_nt(qa, kpad_ref[...]), -jnp.inf)
        m = jnp.max(s, axis=1, keepdims=True)
        p = jnp.exp(s - m)
        l = jnp.sum(p, axis=1, keepdims=True)
        acc = _mm(p.astype(BF16), vpad_ref[...])

        def kv_block(j, carry):
            c0 = pl.multiple_of(j * FOX_TK, FOX_TK)
            return online(_mm_nt(qa, keys(c0)), v_ref[pl.ds(c0, FOX_TK), :], *carry)

        m, l, acc = lax.fori_loop(0, i, kv_block, (m, l, acc))
        s = jnp.where(causal, _mm_nt(qa, keys(r0)), -jnp.inf)
        m, l, acc = online(s, v_ref[pl.ds(r0, FOX_TK), :], m, l, acc)
        o_ref[pl.ds(r0, ATT_TQ), :] = (acc / l).astype(o_ref.dtype)
        return 0

    lax.fori_loop(0, seq // ATT_TQ, q_block, 0)


def _sb_kernel(q_ref, k_ref, v_ref, mk_ref, mv_ref, o_ref, kpad_ref, vpad_ref, *, seq):
    kpad_ref[...] = jnp.zeros_like(kpad_ref)
    vpad_ref[...] = jnp.zeros_like(vpad_ref)
    _load_meta(kpad_ref, mk_ref)
    _load_meta(vpad_ref, mv_ref)

    def later_matrix(n):
        return jnp.where(lax.broadcasted_iota(jnp.int32, (n, n), 0) > lax.broadcasted_iota(jnp.int32, (n, n), 1),
                         1.0, 0.0).astype(BF16)

    u_blk = later_matrix(SB_TK)
    u_meta = later_matrix(LANES)
    meta_valid = lax.broadcasted_iota(jnp.int32, (ATT_TQ, LANES), 1) < N_META
    row = lax.broadcasted_iota(jnp.int32, (ATT_TQ, SB_TK), 0)
    col = lax.broadcasted_iota(jnp.int32, (ATT_TQ, SB_TK), 1)

    def step(q, kblk, vblk, u, mask, r, acc):
        z = _mm_nt(q, kblk)
        e = jnp.log1p(jnp.exp(-jnp.abs(z)))
        log_beta = jnp.minimum(z, 0.0) - e
        log_om = log_beta - z
        if mask is not None:
            log_om = jnp.where(mask, log_om, 0.0)
        hi, lo = _split_bf16(log_om, 2)
        later = _mm(hi, u) + _mm(lo, u)
        w = jnp.exp(log_beta + later + r)
        if mask is not None:
            w = jnp.where(mask, w, 0.0)
        acc = acc + _mm(w.astype(BF16), vblk)
        r = r + jnp.sum(log_om, axis=1, keepdims=True)
        return r, acc

    n_diag = ATT_TQ // SB_TK

    def q_block(i, _):
        r0 = pl.multiple_of(i * ATT_TQ, ATT_TQ)
        q = q_ref[pl.ds(r0, ATT_TQ), :]
        r = jnp.zeros((ATT_TQ, 1), F32)
        acc = jnp.zeros((ATT_TQ, HEAD_DIM), F32)
        for d in reversed(range(n_diag)):
            c0 = pl.multiple_of(r0 + d * SB_TK, SB_TK)
            mask = (col + d * SB_TK) < row
            r, acc = step(q, k_ref[pl.ds(c0, SB_TK), :], v_ref[pl.ds(c0, SB_TK), :], u_blk, mask, r, acc)

        def kv_block(jj, carry):
            c0 = pl.multiple_of((i * n_diag - 1 - jj) * SB_TK, SB_TK)
            return step(q, k_ref[pl.ds(c0, SB_TK), :], v_ref[pl.ds(c0, SB_TK), :], u_blk, None, *carry)

        r, acc = lax.fori_loop(0, i * n_diag, kv_block, (r, acc))
        r, acc = step(q, kpad_ref[...], vpad_ref[...], u_meta, meta_valid, r, acc)
        o_ref[pl.ds(r0, ATT_TQ), :] = acc.astype(o_ref.dtype)
        return 0

    lax.fori_loop(0, seq // ATT_TQ, q_block, 0)


def _head_spec(rows, col_block0):
    return pl.BlockSpec((rows, HEAD_DIM), lambda b, h: (b, col_block0 + h))


def _meta_spec(n_real, col_block0):
    return pl.BlockSpec((N_META, HEAD_DIM), lambda b, h: (n_real // N_META, col_block0 + h))


def _fox_attention(qkv, qc, kc, *, n_batch, seq):
    n_real = n_batch * seq
    shared = pl.BlockSpec((seq, LANES), lambda b, h: (b, 0))
    return pl.pallas_call(
        functools.partial(_fox_kernel, seq=seq),
        grid=(n_batch, N_HEADS),
        in_specs=[_head_spec(seq, 0), shared, _head_spec(seq, N_HEADS), shared, _head_spec(seq, 2 * N_HEADS),
                  _meta_spec(n_real, N_HEADS), pl.BlockSpec((N_META, LANES), lambda b, h: (n_real // N_META, 0)),
                  _meta_spec(n_real, 2 * N_HEADS)],
        out_specs=_head_spec(seq, 0),
        out_shape=jax.ShapeDtypeStruct((n_real, WIDTH), BF16),
        scratch_shapes=[pltpu.VMEM((LANES, 2 * HEAD_DIM), BF16), pltpu.VMEM((LANES, HEAD_DIM), BF16)],
        compiler_params=_params(2),
        name="fox_attention",
    )(qkv, qc, qkv, kc, qkv, qkv, kc, qkv)


def _sb_attention(qkv, *, n_batch, seq):
    n_real = n_batch * seq
    return pl.pallas_call(
        functools.partial(_sb_kernel, seq=seq),
        grid=(n_batch, N_HEADS),
        in_specs=[_head_spec(seq, 0), _head_spec(seq, N_HEADS), _head_spec(seq, 2 * N_HEADS),
                  _meta_spec(n_real, N_HEADS), _meta_spec(n_real, 2 * N_HEADS)],
        out_specs=_head_spec(seq, 0),
        out_shape=jax.ShapeDtypeStruct((n_real, WIDTH), BF16),
        scratch_shapes=[pltpu.VMEM((LANES, HEAD_DIM), BF16), pltpu.VMEM((LANES, HEAD_DIM), BF16)],
        compiler_params=_params(2),
        name="sb_attention",
    )(qkv, qkv, qkv, qkv, qkv)


def _mix_out_kernel(h_ref, of_ref, os_ref, gf_ref, gs_ref, wbf_ref, wbs_ref, wo_ref, o_ref):
    @pl.when(pl.program_id(1) == 0)
    def _():
        o_ref[...] = h_ref[...]

    tf = _mm(of_ref[...], wbf_ref[...].astype(BF16))
    ts = _mm(os_ref[...], wbs_ref[...].astype(BF16))
    merged = gf_ref[...].astype(F32) * tf + gs_ref[...].astype(F32) * ts
    o_ref[...] += _mm(merged.astype(BF16), wo_ref[...].astype(BF16))


def _mix_out(h, o_fox, o_sb, gates, wbf, wbs, wo, *, rows, tm, tn):
    ni, nj = rows // tm, D_MODEL // tn
    return pl.pallas_call(
        _mix_out_kernel,
        grid=(ni, nj),
        in_specs=[
            pl.BlockSpec((tm, D_MODEL), lambda i, j: (i, 0)),
            pl.BlockSpec((tm, WIDTH), lambda i, j: (i, 0)),
            pl.BlockSpec((tm, WIDTH), lambda i, j: (i, 0)),
            pl.BlockSpec((tm, tn), lambda i, j: (i, j)),
            pl.BlockSpec((tm, tn), lambda i, j: (i, nj + j)),
            pl.BlockSpec((WIDTH, tn), lambda i, j: (0, j)),
            pl.BlockSpec((WIDTH, tn), lambda i, j: (0, j)),
            pl.BlockSpec((tn, D_MODEL), lambda i, j: (j, 0)),
        ],
        out_specs=pl.BlockSpec((tm, D_MODEL), lambda i, j: (i, 0)),
        out_shape=jax.ShapeDtypeStruct((rows, D_MODEL), F32),
        compiler_params=_params(2),
        name="mix_out",
    )(h, o_fox, o_sb, gates, gates, wbf, wbs, wo)


def kernel(x, meta_tokens, ffn1_norm, ffn1_w_gate, ffn1_w_up, ffn1_w_down, mix_norm, w_in, b_forget, fox_q_norm, fox_k_norm, w_branch_fox, w_branch_sb, w_out, ffn2_norm, ffn2_w_gate, ffn2_w_up, ffn2_w_down):
    n_batch, seq, _ = x.shape
    n_real = n_batch * seq
    n_all = n_real + N_META
    assert ffn1_norm.shape[0] == 1
    assert n_all % ROW_TILE_ALL == 0 and n_real % ROW_TILE_REAL == 0 and seq % ATT_TQ == 0

    c_f = 3 * WIDTH
    c_sb = c_f + N_HEADS
    w_layer = w_in[0]
    w_tail = w_layer[:, c_sb:]
    w_f = jnp.pad(w_layer[:, c_f:c_sb], ((0, 0), (0, LANES - N_HEADS)))
    b_f = jnp.pad(b_forget[0].reshape(1, N_HEADS), ((0, 0), (0, LANES - N_HEADS)))
    fox_gain = jnp.concatenate([fox_q_norm[0].reshape(1, WIDTH) * SCALE, fox_k_norm[0].reshape(1, WIDTH),
                                jnp.ones((1, WIDTH), F32)], axis=1)
    sb_scale = jnp.concatenate([jnp.full((1, WIDTH), SCALE, F32), jnp.ones((1, 2 * WIDTH), F32)], axis=1)

    h1, n1 = _ffn(x.reshape(n_real, D_MODEL), meta_tokens.astype(F32), ffn1_norm[0], ffn1_w_gate[0],
                  ffn1_w_up[0], ffn1_w_down[0], rows=n_all, tm=ROW_TILE_ALL, next_gain=mix_norm[0])

    proj = functools.partial(_proj, n1, rows=n_all, tm=ROW_TILE_ALL)
    fox_qkv = proj(w_layer, 0, 3 * WIDTH, fox_gain, tn=PROJ_TILE_N, epilogue=_fox_epilogue,
                   out_dtype=BF16, name="proj_fox")
    log_f = proj(w_f, 0, LANES, b_f, tn=LANES, epilogue=_logf_epilogue, out_dtype=F32, name="proj_logf")
    sb_qkv = proj(w_tail, 0, 3 * WIDTH, sb_scale, tn=PROJ_TILE_N, epilogue=_scale_epilogue,
                  out_dtype=BF16, name="proj_sb")
    gates = _proj(n1, w_tail, 3 * WIDTH, 2 * D_MODEL, None, rows=n_real, tm=ROW_TILE_REAL,
                  tn=PROJ_TILE_N, epilogue=_gate_epilogue, out_dtype=BF16, name="proj_gates")

    qc, kc = _decay(log_f, n_batch=n_batch, seq=seq)
    o_fox = _fox_attention(fox_qkv, qc, kc, n_batch=n_batch, seq=seq)
    o_sb = _sb_attention(sb_qkv, n_batch=n_batch, seq=seq)

    h2 = _mix_out(h1, o_fox, o_sb, gates, w_branch_fox[0], w_branch_sb[0], w_out[0],
                  rows=n_real, tm=ROW_TILE_REAL, tn=OUT_TILE_N)
    h3 = _ffn(h2, None, ffn2_norm[0], ffn2_w_gate[0], ffn2_w_up[0], ffn2_w_down[0],
              rows=n_real, tm=ROW_TILE_REAL)
    return h3.reshape(n_batch, seq, D_MODEL)
```

```python
import functools
import math

import jax
import jax.numpy as jnp
from jax import lax
from jax.experimental import pallas as pl
from jax.experimental.pallas import tpu as pltpu

F32 = jnp.float32
BF16 = jnp.bfloat16

D_MODEL = 2048
D_FF = 5632
N_META = 16
HEAD_DIM = 128
N_HEADS = 8
WIDTH = N_HEADS * HEAD_DIM
RMS_EPS = 1e-6
FFN_RESIDUAL_WEIGHT = 0.5
SCALE = HEAD_DIM ** -0.5
LOG2E = math.log2(math.e)

LANES = 128
SUBLANES = 8
VMEM_LIMIT = 58 * 2**20

ROW_TILE_ALL = 912
ROW_TILE_REAL = 1024
FFN_TILE_F = 256
PROJ_TILE_N = 1024
OUT_TILE_N = 256
ATT_TQ = 512
ATT_HALF = ATT_TQ // 2
FOX_TK = 256
SB_TK = 256
DECAY_TERMS = 3
MAX_EXP2 = 126.0


def _mm(a, b):
    return jnp.dot(a, b, preferred_element_type=F32)


def _mm_nt(a, b):
    return lax.dot_general(a, b, (((1,), (1,)), ((), ())), preferred_element_type=F32)


def _split_bf16(x, n):
    parts = [x.astype(BF16)]
    for _ in range(n - 1):
        x = x - parts[-1].astype(F32)
        parts.append(x.astype(BF16))
    return parts


def _rmsnorm_rows(h, gain):
    ms = jnp.mean(h * h, axis=-1, keepdims=True)
    return h * lax.rsqrt(ms + RMS_EPS) * gain


def _log_sigmoid(z):
    return jnp.minimum(z, 0.0) - jnp.log(1.0 + jnp.exp(-jnp.abs(z)))


def _params(n_grid_axes):
    return pltpu.CompilerParams(dimension_semantics=("arbitrary",) * n_grid_axes,
                                vmem_limit_bytes=VMEM_LIMIT)


def _ffn_kernel(*refs, meta_row, emit_norm):
    refs = list(refs)
    x_ref = refs.pop(0)
    meta_ref = refs.pop(0) if meta_row is not None else None
    gain_ref, wg_ref, wu_ref, wd_ref = refs[:4]
    refs = refs[4:]
    if emit_norm:
        ngain_ref, o_ref, n_ref, xn_ref = refs
    else:
        o_ref, xn_ref = refs
    i = pl.program_id(0)
    j = pl.program_id(1)

    @pl.when(j == 0)
    def _():
        if meta_row is None:
            o_ref[...] = x_ref[...]
        else:
            last = pl.num_programs(0) - 1

            @pl.when(i < last)
            def _():
                o_ref[...] = x_ref[...]

            @pl.when(i == last)
            def _():
                o_ref[0:meta_row, :] = x_ref[0:meta_row, :]
                o_ref[meta_row:meta_row + N_META, :] = meta_ref[...]
        xn_ref[...] = _rmsnorm_rows(o_ref[...], gain_ref[...]).astype(BF16)

    xn = xn_ref[...]
    g = _mm(xn, wg_ref[...].astype(BF16))
    u = _mm(xn, wu_ref[...].astype(BF16))
    a = (g * jax.nn.sigmoid(g)) * (u * FFN_RESIDUAL_WEIGHT)
    o_ref[...] += _mm(a.astype(BF16), wd_ref[...].astype(BF16))

    if emit_norm:
        @pl.when(j == pl.num_programs(1) - 1)
        def _():
            n_ref[...] = _rmsnorm_rows(o_ref[...], ngain_ref[...]).astype(BF16)


def _ffn(x, meta, gain, wg, wu, wd, *, rows, tm, next_gain=None):
    ni = rows // tm
    nj = D_FF // FFN_TILE_F
    emit_norm = next_gain is not None
    in_specs = [pl.BlockSpec((tm, D_MODEL), lambda i, j: (i, 0))]
    args = [x]
    meta_row = None
    if meta is not None:
        meta_row = x.shape[0] - (ni - 1) * tm
        assert meta_row + N_META == tm
        in_specs.append(pl.BlockSpec((N_META, D_MODEL), lambda i, j: (0, 0)))
        args.append(meta)
    in_specs += [
        pl.BlockSpec((1, D_MODEL), lambda i, j: (0, 0)),
        pl.BlockSpec((D_MODEL, FFN_TILE_F), lambda i, j: (0, j)),
        pl.BlockSpec((D_MODEL, FFN_TILE_F), lambda i, j: (0, j)),
        pl.BlockSpec((FFN_TILE_F, D_MODEL), lambda i, j: (j, 0)),
    ]
    args += [gain.reshape(1, D_MODEL), wg, wu, wd]
    out_shape = [jax.ShapeDtypeStruct((rows, D_MODEL), F32)]
    out_specs = [pl.BlockSpec((tm, D_MODEL), lambda i, j: (i, 0))]
    if emit_norm:
        in_specs.append(pl.BlockSpec((1, D_MODEL), lambda i, j: (0, 0)))
        args.append(next_gain.reshape(1, D_MODEL))
        out_shape.append(jax.ShapeDtypeStruct((rows, D_MODEL), BF16))
        out_specs.append(pl.BlockSpec((tm, D_MODEL), lambda i, j: (i, 0)))
    res = pl.pallas_call(
        functools.partial(_ffn_kernel, meta_row=meta_row, emit_norm=emit_norm),
        grid=(ni, nj),
        in_specs=in_specs,
        out_specs=out_specs,
        out_shape=out_shape,
        scratch_shapes=[pltpu.VMEM((tm, D_MODEL), BF16)],
        compiler_params=_params(2),
        name="ffn_norm" if emit_norm else "ffn",
    )(*args)
    return res if emit_norm else res[0]


def _proj_kernel(n_ref, w_ref, *refs, epilogue):
    p_ref = refs[0] if len(refs) == 3 else None
    o_ref, wb_ref = refs[-2:]

    @pl.when(pl.program_id(1) == 0)
    def _():
        wb_ref[...] = w_ref[...].T.astype(BF16)

    y = _mm(n_ref[...], wb_ref[...])
    epilogue(y, p_ref, o_ref, pl.program_id(0))


def _fox_epilogue(y, p_ref, o_ref, j):
    @pl.when(j < 2)
    def _():
        for h in range(y.shape[1] // HEAD_DIM):
            sl = slice(h * HEAD_DIM, (h + 1) * HEAD_DIM)
            o_ref[:, sl] = _rmsnorm_rows(y[:, sl], p_ref[:, sl]).astype(o_ref.dtype)

    @pl.when(j >= 2)
    def _():
        o_ref[...] = y.astype(o_ref.dtype)


def _scale_epilogue(y, p_ref, o_ref, j):
    o_ref[...] = (y * p_ref[...]).astype(o_ref.dtype)


def _gate_epilogue(y, p_ref, o_ref, j):
    o_ref[...] = jax.nn.sigmoid(y).astype(o_ref.dtype)


def _logf_epilogue(y, p_ref, o_ref, j):
    o_ref[...] = (_log_sigmoid(y + p_ref[...]) * LOG2E).astype(o_ref.dtype)


def _proj(n, wt, row0, n_cols, p, *, rows, tm, tn, epilogue, out_dtype, name):
    assert n_cols % tn == 0
    nj, ni = n_cols // tn, rows // tm
    if row0 % tn == 0:
        w_spec = pl.BlockSpec((tn, D_MODEL), lambda j, i: (row0 // tn + j, 0))
    else:
        assert row0 % SUBLANES == 0 and tn % SUBLANES == 0
        w_spec = pl.BlockSpec((pl.Element(tn), pl.Element(D_MODEL)),
                              lambda j, i: ((row0 // SUBLANES + j * (tn // SUBLANES)) * SUBLANES, 0))
    in_specs = [pl.BlockSpec((tm, D_MODEL), lambda j, i: (i, 0)), w_spec]
    args = [n, wt]
    if p is not None:
        in_specs.append(pl.BlockSpec((1, tn), lambda j, i: (0, j)))
        args.append(p)
    return pl.pallas_call(
        functools.partial(_proj_kernel, epilogue=epilogue),
        grid=(nj, ni),
        in_specs=in_specs,
        out_specs=pl.BlockSpec((tm, tn), lambda j, i: (i, j)),
        out_shape=jax.ShapeDtypeStruct((rows, n_cols), out_dtype),
        scratch_shapes=[pltpu.VMEM((D_MODEL, tn), BF16)],
        compiler_params=_params(2),
        name=name,
    )(*args)


def _decay_kernel(lf_ref, qc_ref, kc_ref, c_ref, *, n_batch, seq):
    T = DECAY_TERMS
    n_real = n_batch * seq
    r_i = lax.broadcasted_iota(jnp.int32, (LANES, LANES), 0)
    c_i = lax.broadcasted_iota(jnp.int32, (LANES, LANES), 1)
    tri = jnp.where(r_i >= c_i, 1.0, 0.0).astype(BF16)

    def cum_block(x, carry):
        acc = carry
        for p in _split_bf16(x, T):
            acc = acc + _mm(tri, p)
        return acc

    xm = jnp.concatenate([lf_ref[n_real:n_real + N_META, :], jnp.zeros((LANES - N_META, LANES), F32)], axis=0)
    cm = cum_block(xm, jnp.zeros((1, LANES), F32))
    c_ref[n_real:n_real + N_META, :] = cm[:N_META, :]
    base = cm[N_META - 1:N_META, :]

    for b in range(n_batch):
        def body(t, carry, b=b):
            r0 = pl.multiple_of(b * seq + t * LANES, LANES)
            c = cum_block(lf_ref[pl.ds(r0, LANES), :], carry)
            c_ref[pl.ds(r0, LANES), :] = c
            return c[LANES - 1:LANES, :]
        lax.fori_loop(0, seq // LANES, body, base)

    k_i = lax.broadcasted_iota(jnp.int32, (T * LANES, LANES), 0)
    n_i = lax.broadcasted_iota(jnp.int32, (T * LANES, LANES), 1)
    sel_q = jnp.zeros((T * LANES, LANES), F32)
    sel_k = jnp.zeros((T * LANES, LANES), F32)
    lane = lax.broadcasted_iota(jnp.int32, (1, LANES), 1)
    ones_q = jnp.zeros((1, LANES), F32)
    ones_k = jnp.zeros((1, LANES), F32)
    for h in range(N_HEADS):
        for p in range(T):
            row = p * LANES + h
            sel_q = jnp.where((k_i == row) & (n_i == 2 * T * h + p), 1.0, sel_q)
            sel_k = jnp.where((k_i == row) & (n_i == 2 * T * h + T + p), -1.0, sel_k)
            ones_q = jnp.where(lane == 2 * T * h + T + p, 1.0, ones_q)
            ones_k = jnp.where(lane == 2 * T * h + p, 1.0, ones_k)
    sel_q = sel_q.astype(BF16)
    sel_k = sel_k.astype(BF16)

    def expand(c):
        pieces = jnp.concatenate(_split_bf16(c, T), axis=1)
        return ((_mm(pieces, sel_q) + ones_q).astype(BF16), (_mm(pieces, sel_k) + ones_k).astype(BF16))

    chunk = 1024
    def ebody(t, _):
        r0 = pl.multiple_of(t * chunk, chunk)
        q, k = expand(c_ref[pl.ds(r0, chunk), :])
        qc_ref[pl.ds(r0, chunk), :] = q
        kc_ref[pl.ds(r0, chunk), :] = k
        return 0
    lax.fori_loop(0, n_real // chunk, ebody, 0)
    _, km = expand(c_ref[n_real:n_real + N_META, :])
    kc_ref[n_real:n_real + N_META, :] = km


def _decay(lf, *, n_batch, seq):
    n_real = n_batch * seq
    return pl.pallas_call(
        functools.partial(_decay_kernel, n_batch=n_batch, seq=seq),
        out_shape=[jax.ShapeDtypeStruct((n_real, LANES), BF16),
                   jax.ShapeDtypeStruct((n_real + N_META, LANES), BF16)],
        scratch_shapes=[pltpu.VMEM((n_real + N_META, LANES), F32)],
        compiler_params=pltpu.CompilerParams(vmem_limit_bytes=VMEM_LIMIT),
        name="decay",
    )(lf)


def _eye(n):
    return jnp.where(lax.broadcasted_iota(jnp.int32, (n, n), 0) == lax.broadcasted_iota(jnp.int32, (n, n), 1),
                     1.0, 0.0).astype(BF16)


def _stage_keys_values(kpad_ref, vt_ref, meta_k_refs, v_ref, mv_ref, *, seq):
    eye = _eye(HEAD_DIM)
    kpad_ref[...] = jnp.zeros_like(kpad_ref)
    for n, ref in enumerate(meta_k_refs):
        kpad_ref[0:N_META, n * HEAD_DIM:(n + 1) * HEAD_DIM] = ref[...]
    vpad = jnp.concatenate([mv_ref[...], jnp.zeros((LANES - N_META, HEAD_DIM), BF16)], axis=0)
    vt_ref[:, seq:seq + LANES] = _mm_nt(eye, vpad).astype(BF16)
    chunk = 512

    def body(c, _):
        c0 = pl.multiple_of(c * chunk, chunk)
        vt_ref[:, pl.ds(c0, chunk)] = _mm_nt(eye, v_ref[pl.ds(c0, chunk), :]).astype(BF16)
        return 0
    lax.fori_loop(0, seq // chunk, body, 0)


def _store_transposed(o_ref, r0, acc_t):
    o_ref[pl.ds(r0, ATT_HALF), :] = acc_t.T.astype(o_ref.dtype)


def _fox_kernel(q_ref, qc_ref, k_ref, kc_ref, v_ref, mk_ref, mkc_ref, mv_ref, o_ref, kpad_ref, vt_ref,
                sa_ref, sb_ref, *, seq):
    assert ATT_TQ == 2 * FOX_TK and FOX_TK == ATT_HALF
    T = DECAY_TERMS
    h = pl.program_id(1)
    _stage_keys_values(kpad_ref, vt_ref, (mk_ref, mkc_ref), v_ref, mv_ref, seq=seq)

    lane = lax.broadcasted_iota(jnp.int32, (1, LANES), 1)
    head_lanes = (lane >= 2 * T * h) & (lane < 2 * T * (h + 1))
    meta_valid = lax.broadcasted_iota(jnp.int32, (LANES, ATT_HALF), 0) < N_META
    causal = (lax.broadcasted_iota(jnp.int32, (ATT_HALF, ATT_HALF), 0)
              <= lax.broadcasted_iota(jnp.int32, (ATT_HALF, ATT_HALF), 1))

    def keys(c0, n):
        return jnp.concatenate([k_ref[pl.ds(c0, n), :], kc_ref[pl.ds(c0, n), :]], axis=1)

    def online(s, vt, m, l, acc):
        m_new = jnp.maximum(m, jnp.max(s, axis=0, keepdims=True))
        alpha = jnp.exp2(m - m_new)
        p = jnp.exp2(s - m_new)
        l = alpha * l + jnp.sum(p, axis=0, keepdims=True)
        acc = alpha * acc + _mm(vt, p.astype(BF16))
        return m_new, l, acc

    def q_block(i, _):
        r0 = pl.multiple_of(i * ATT_TQ, ATT_TQ)
        qx = jnp.where(head_lanes, qc_ref[pl.ds(r0, ATT_TQ), :], jnp.zeros((), BF16))
        qa = jnp.concatenate([q_ref[pl.ds(r0, ATT_TQ), :], qx], axis=1)
        halves = (qa[:ATT_HALF], qa[ATT_HALF:])

        state = []
        for qh in halves:
            s = jnp.where(meta_valid, _mm_nt(kpad_ref[...], qh), -jnp.inf)
            m = jnp.max(s, axis=0, keepdims=True)
            p = jnp.exp2(s - m)
            state.append((m, jnp.sum(p, axis=0, keepdims=True), _mm(vt_ref[:, seq:seq + LANES], p.astype(BF16))))

        def issue(buf_ref, c0, which=(0, 1)):
            ka = keys(c0, FOX_TK)
            for half in which:
                buf_ref[half] = _mm_nt(ka, halves[half])

        def consume(buf_ref, c0, state, masks=(None, None), which=(0, 1)):
            vt = vt_ref[:, pl.ds(c0, FOX_TK)]
            out = list(state)
            for half in which:
                s = buf_ref[half]
                if masks[half] is not None:
                    s = jnp.where(masks[half], s, -jnp.inf)
                out[half] = online(s, vt, *out[half])
            return tuple(out)

        issue(sa_ref, 0)

        def kv_pair(t, state):
            c0 = pl.multiple_of(t * (2 * FOX_TK), 2 * FOX_TK)
            c1 = pl.multiple_of(c0 + FOX_TK, FOX_TK)
            issue(sb_ref, c1)
            state = consume(sa_ref, c0, state)
            issue(sa_ref, pl.multiple_of(c0 + 2 * FOX_TK, 2 * FOX_TK))
            return consume(sb_ref, c1, state)

        state = lax.fori_loop(0, i, kv_pair, tuple(state))

        r1 = pl.multiple_of(r0 + ATT_HALF, ATT_HALF)
        issue(sb_ref, r1, which=(1,))
        state = consume(sa_ref, r0, state, masks=(causal, None))
        s0, s1 = consume(sb_ref, r1, state, masks=(None, causal), which=(1,))
        for rr, (m, l, acc) in ((r0, s0), (r1, s1)):
            _store_transposed(o_ref, rr, acc / l)
        return 0

    lax.fori_loop(0, seq // ATT_TQ, q_block, 0)


def _sb_kernel(q_ref, k_ref, v_ref, mk_ref, mv_ref, o_ref, kpad_ref, vt_ref, z_ref, *, seq):
    _stage_keys_values(kpad_ref, vt_ref, (mk_ref,), v_ref, mv_ref, seq=seq)

    def from_here_on(n):
        return jnp.where(lax.broadcasted_iota(jnp.int32, (n, n), 0) <= lax.broadcasted_iota(jnp.int32, (n, n), 1),
                         1.0, 0.0).astype(BF16)

    u_blk = from_here_on(SB_TK)
    u_meta = from_here_on(LANES)
    meta_valid = lax.broadcasted_iota(jnp.int32, (LANES, ATT_HALF), 0) < N_META
    strict = (lax.broadcasted_iota(jnp.int32, (SB_TK, ATT_HALF), 0)
              < lax.broadcasted_iota(jnp.int32, (SB_TK, ATT_HALF), 1))

    def steps(items, states, zs=None):
        states = list(states)
        if zs is None:
            zs = [_mm_nt(kblk, qh) for _, qh, kblk, _, _, _ in items]
        cums = []
        for z, (_, _, _, _, u, mask) in zip(zs, items):
            sp = jnp.maximum(jnp.log2(1.0 + jnp.exp2(jnp.minimum(z, MAX_EXP2))), z)
            if mask is not None:
                sp = jnp.where(mask, sp, 0.0)
            hi, lo = _split_bf16(sp, 2)
            cums.append(_mm(u, hi) + _mm(u, lo))
        for z, cum, (half, _, _, vt, _, mask) in zip(zs, cums, items):
            r, acc = states[half]
            w = jnp.exp2(z - cum - r)
            if mask is not None:
                w = jnp.where(mask, w, 0.0)
            states[half] = (r + cum[0:1, :], acc + _mm(vt, w.astype(BF16)))
        return tuple(states)

    def q_block(i, _):
        r0 = pl.multiple_of(i * ATT_TQ, ATT_TQ)
        r1 = pl.multiple_of(r0 + ATT_HALF, ATT_HALF)
        q = (q_ref[pl.ds(r0, ATT_HALF), :], q_ref[pl.ds(r1, ATT_HALF), :])
        zero = (jnp.zeros((1, ATT_HALF), F32), jnp.zeros((HEAD_DIM, ATT_HALF), F32))

        def item(half, c0, mask):
            return (half, q[half], k_ref[pl.ds(c0, SB_TK), :], vt_ref[:, pl.ds(c0, SB_TK)], u_blk, mask)

        states = steps([item(1, r1, strict), item(0, r0, strict), item(1, r0, None)], (zero, zero))

        def pair_items(jj):
            items = []
            for d in (1, 2):
                c0 = pl.multiple_of(jnp.maximum(r0 - (2 * jj + d) * SB_TK, 0), SB_TK)
                items += [item(0, c0, None), item(1, c0, None)]
            return items

        def scores(items):
            return [_mm_nt(kblk, qh) for _, qh, kblk, _, _, _ in items]

        def kv_pair(jj, states):
            zs = [z_ref[n] for n in range(4)]
            ahead = scores(pair_items(jj + 1))
            states = steps(pair_items(jj), states, zs)
            for n, z in enumerate(ahead):
                z_ref[n] = z
            return states

        for n, z in enumerate(scores(pair_items(0))):
            z_ref[n] = z
        states = lax.fori_loop(0, i, kv_pair, states)
        meta = [(half, q[half], kpad_ref[...], vt_ref[:, seq:seq + LANES], u_meta, meta_valid) for half in (0, 1)]
        states = steps(meta, states)
        for rr, (_, acc) in zip((r0, r1), states):
            _store_transposed(o_ref, rr, acc)
        return 0

    lax.fori_loop(0, seq // ATT_TQ, q_block, 0)


def _head_spec(rows, col_block0):
    return pl.BlockSpec((rows, HEAD_DIM), lambda b, h: (b, col_block0 + h))


def _meta_spec(n_real, col_block0):
    return pl.BlockSpec((N_META, HEAD_DIM), lambda b, h: (n_real // N_META, col_block0 + h))


def _fox_attention(qkv, qc, kc, *, n_batch, seq):
    n_real = n_batch * seq
    shared = pl.BlockSpec((seq, LANES), lambda b, h: (b, 0))
    return pl.pallas_call(
        functools.partial(_fox_kernel, seq=seq),
        grid=(n_batch, N_HEADS),
        in_specs=[_head_spec(seq, 0), shared, _head_spec(seq, N_HEADS), shared, _head_spec(seq, 2 * N_HEADS),
                  _meta_spec(n_real, N_HEADS), pl.BlockSpec((N_META, LANES), lambda b, h: (n_real // N_META, 0)),
                  _meta_spec(n_real, 2 * N_HEADS)],
        out_specs=_head_spec(seq, 0),
        out_shape=jax.ShapeDtypeStruct((n_real, WIDTH), BF16),
        scratch_shapes=[pltpu.VMEM((LANES, 2 * HEAD_DIM), BF16), pltpu.VMEM((HEAD_DIM, seq + LANES), BF16),
                        pltpu.VMEM((2, FOX_TK, ATT_HALF), F32), pltpu.VMEM((2, FOX_TK, ATT_HALF), F32)],
        compiler_params=_params(2),
        name="fox_attention",
    )(qkv, qc, qkv, kc, qkv, qkv, kc, qkv)


def _sb_attention(qkv, *, n_batch, seq):
    n_real = n_batch * seq
    return pl.pallas_call(
        functools.partial(_sb_kernel, seq=seq),
        grid=(n_batch, N_HEADS),
        in_specs=[_head_spec(seq, 0), _head_spec(seq, N_HEADS), _head_spec(seq, 2 * N_HEADS),
                  _meta_spec(n_real, N_HEADS), _meta_spec(n_real, 2 * N_HEADS)],
        out_specs=_head_spec(seq, 0),
        out_shape=jax.ShapeDtypeStruct((n_real, WIDTH), BF16),
        scratch_shapes=[pltpu.VMEM((LANES, HEAD_DIM), BF16), pltpu.VMEM((HEAD_DIM, seq + LANES), BF16),
                        pltpu.VMEM((4, SB_TK, ATT_HALF), F32)],
        compiler_params=_params(2),
        name="sb_attention",
    )(qkv, qkv, qkv, qkv, qkv)


def _mix_out_kernel(h_ref, of_ref, os_ref, gf_ref, gs_ref, wbf_ref, wbs_ref, wo_ref, o_ref):
    @pl.when(pl.program_id(1) == 0)
    def _():
        o_ref[...] = h_ref[...]

    tf = _mm(of_ref[...], wbf_ref[...].astype(BF16))
    ts = _mm(os_ref[...], wbs_ref[...].astype(BF16))
    merged = gf_ref[...].astype(F32) * tf + gs_ref[...].astype(F32) * ts
    o_ref[...] += _mm(merged.astype(BF16), wo_ref[...].astype(BF16))


def _mix_out(h, o_fox, o_sb, gates, wbf, wbs, wo, *, rows, tm, tn):
    ni, nj = rows // tm, D_MODEL // tn
    return pl.pallas_call(
        _mix_out_kernel,
        grid=(ni, nj),
        in_specs=[
            pl.BlockSpec((tm, D_MODEL), lambda i, j: (i, 0)),
            pl.BlockSpec((tm, WIDTH), lambda i, j: (i, 0)),
            pl.BlockSpec((tm, WIDTH), lambda i, j: (i, 0)),
            pl.BlockSpec((tm, tn), lambda i, j: (i, j)),
            pl.BlockSpec((tm, tn), lambda i, j: (i, nj + j)),
            pl.BlockSpec((WIDTH, tn), lambda i, j: (0, j)),
            pl.BlockSpec((WIDTH, tn), lambda i, j: (0, j)),
            pl.BlockSpec((tn, D_MODEL), lambda i, j: (j, 0)),
        ],
        out_specs=pl.BlockSpec((tm, D_MODEL), lambda i, j: (i, 0)),
        out_shape=jax.ShapeDtypeStruct((rows, D_MODEL), F32),
        compiler_params=_params(2),
        name="mix_out",
    )(h, o_fox, o_sb, gates, gates, wbf, wbs, wo)


def kernel(x, meta_tokens, ffn1_norm, ffn1_w_gate, ffn1_w_up, ffn1_w_down, mix_norm, w_in, b_forget, fox_q_norm, fox_k_norm, w_branch_fox, w_branch_sb, w_out, ffn2_norm, ffn2_w_gate, ffn2_w_up, ffn2_w_down):
    n_batch, seq, _ = x.shape
    n_real = n_batch * seq
    n_all = n_real + N_META
    assert ffn1_norm.shape[0] == 1
    assert n_all % ROW_TILE_ALL == 0 and n_real % ROW_TILE_REAL == 0 and seq % ATT_TQ == 0

    wt = w_in[0].T
    r_f = 3 * WIDTH
    r_sb = r_f + N_HEADS
    r_gate = r_sb + 3 * WIDTH
    wt_f = jnp.pad(wt[r_f:r_sb], ((0, LANES - N_HEADS), (0, 0)))
    b_f = jnp.pad(b_forget[0].reshape(1, N_HEADS), ((0, 0), (0, LANES - N_HEADS)))
    fox_gain = jnp.concatenate([fox_q_norm[0].reshape(1, WIDTH) * (SCALE * LOG2E), fox_k_norm[0].reshape(1, WIDTH),
                                jnp.ones((1, WIDTH), F32)], axis=1)
    sb_scale = jnp.concatenate([jnp.full((1, WIDTH), SCALE * LOG2E, F32), jnp.ones((1, 2 * WIDTH), F32)], axis=1)

    h1, n1 = _ffn(x.reshape(n_real, D_MODEL), meta_tokens.astype(F32), ffn1_norm[0], ffn1_w_gate[0],
                  ffn1_w_up[0], ffn1_w_down[0], rows=n_all, tm=ROW_TILE_ALL, next_gain=mix_norm[0])

    proj = functools.partial(_proj, n1, rows=n_all, tm=ROW_TILE_ALL)
    fox_qkv = proj(wt, 0, 3 * WIDTH, fox_gain, tn=PROJ_TILE_N, epilogue=_fox_epilogue,
                   out_dtype=BF16, name="proj_fox")
    log_f = proj(wt_f, 0, LANES, b_f, tn=LANES, epilogue=_logf_epilogue, out_dtype=F32, name="proj_logf")
    sb_qkv = proj(wt, r_sb, 3 * WIDTH, sb_scale, tn=PROJ_TILE_N, epilogue=_scale_epilogue,
                  out_dtype=BF16, name="proj_sb")
    gates = _proj(n1, wt, r_gate, 2 * D_MODEL, None, rows=n_real, tm=ROW_TILE_REAL,
                  tn=PROJ_TILE_N, epilogue=_gate_epilogue, out_dtype=BF16, name="proj_gates")

    qc, kc = _decay(log_f, n_batch=n_batch, seq=seq)
    o_fox = _fox_attention(fox_qkv, qc, kc, n_batch=n_batch, seq=seq)
    o_sb = _sb_attention(sb_qkv, n_batch=n_batch, seq=seq)

    h2 = _mix_out(h1, o_fox, o_sb, gates, w_branch_fox[0], w_branch_sb[0], w_out[0],
                  rows=n_real, tm=ROW_TILE_REAL, tn=OUT_TILE_N)
    h3 = _ffn(h2, None, ffn2_norm[0], ffn2_w_gate[0], ffn2_w_up[0], ffn2_w_down[0],
              rows=n_real, tm=ROW_TILE_REAL)
    return h3.reshape(n_batch, seq, D_MODEL)
```

```python
import functools
import math

import jax
import jax.numpy as jnp
from jax import lax
from jax.experimental import pallas as pl
from jax.experimental.pallas import tpu as pltpu

F32 = jnp.float32
BF16 = jnp.bfloat16

D_MODEL = 2048
D_FF = 5632
N_META = 16
HEAD_DIM = 128
N_HEADS = 8
WIDTH = N_HEADS * HEAD_DIM
RMS_EPS = 1e-6
FFN_RESIDUAL_WEIGHT = 0.5
SCALE = HEAD_DIM ** -0.5
LOG2E = math.log2(math.e)

LANES = 128
SUBLANES = 8
VMEM_LIMIT = 58 * 2**20

ROW_TILE_ALL = 912
ROW_TILE_REAL = 1024
FFN_TILE_F = 256
PROJ_TILE_N = 1024
OUT_TILE_N = 256
ATT_SUB = 256
ATT_NSUB = 4
ATT_TQ = ATT_SUB * ATT_NSUB
DECAY_TERMS = 3
MAX_EXP2 = 126.0


def _mm(a, b):
    return jnp.dot(a, b, preferred_element_type=F32)


def _mm_nt(a, b):
    return lax.dot_general(a, b, (((1,), (1,)), ((), ())), preferred_element_type=F32)


def _split_bf16(x, n):
    parts = [x.astype(BF16)]
    for _ in range(n - 1):
        x = x - parts[-1].astype(F32)
        parts.append(x.astype(BF16))
    return parts


def _rmsnorm_rows(h, gain):
    ms = jnp.mean(h * h, axis=-1, keepdims=True)
    return h * lax.rsqrt(ms + RMS_EPS) * gain


def _log_sigmoid(z):
    return jnp.minimum(z, 0.0) - jnp.log(1.0 + jnp.exp(-jnp.abs(z)))


def _params(n_grid_axes):
    return pltpu.CompilerParams(dimension_semantics=("arbitrary",) * n_grid_axes,
                                vmem_limit_bytes=VMEM_LIMIT)


def _ffn_kernel(*refs, meta_row, emit_norm):
    refs = list(refs)
    x_ref = refs.pop(0)
    meta_ref = refs.pop(0) if meta_row is not None else None
    gain_ref, wg_ref, wu_ref, wd_ref = refs[:4]
    refs = refs[4:]
    if emit_norm:
        ngain_ref, o_ref, n_ref, xn_ref = refs
    else:
        o_ref, xn_ref = refs
    i = pl.program_id(0)
    j = pl.program_id(1)

    @pl.when(j == 0)
    def _():
        if meta_row is None:
            o_ref[...] = x_ref[...]
        else:
            last = pl.num_programs(0) - 1

            @pl.when(i < last)
            def _():
                o_ref[...] = x_ref[...]

            @pl.when(i == last)
            def _():
                o_ref[0:meta_row, :] = x_ref[0:meta_row, :]
                o_ref[meta_row:meta_row + N_META, :] = meta_ref[...]
        xn_ref[...] = _rmsnorm_rows(o_ref[...], gain_ref[...]).astype(BF16)

    xn = xn_ref[...]
    g = _mm(xn, wg_ref[...].astype(BF16))
    u = _mm(xn, wu_ref[...].astype(BF16))
    a = (g * jax.nn.sigmoid(g)) * (u * FFN_RESIDUAL_WEIGHT)
    o_ref[...] += _mm(a.astype(BF16), wd_ref[...].astype(BF16))

    if emit_norm:
        @pl.when(j == pl.num_programs(1) - 1)
        def _():
            n_ref[...] = _rmsnorm_rows(o_ref[...], ngain_ref[...]).astype(BF16)


def _ffn(x, meta, gain, wg, wu, wd, *, rows, tm, next_gain=None):
    ni = rows // tm
    nj = D_FF // FFN_TILE_F
    emit_norm = next_gain is not None
    in_specs = [pl.BlockSpec((tm, D_MODEL), lambda i, j: (i, 0))]
    args = [x]
    meta_row = None
    if meta is not None:
        meta_row = x.shape[0] - (ni - 1) * tm
        assert meta_row + N_META == tm
        in_specs.append(pl.BlockSpec((N_META, D_MODEL), lambda i, j: (0, 0)))
        args.append(meta)
    in_specs += [
        pl.BlockSpec((1, D_MODEL), lambda i, j: (0, 0)),
        pl.BlockSpec((D_MODEL, FFN_TILE_F), lambda i, j: (0, j)),
        pl.BlockSpec((D_MODEL, FFN_TILE_F), lambda i, j: (0, j)),
        pl.BlockSpec((FFN_TILE_F, D_MODEL), lambda i, j: (j, 0)),
    ]
    args += [gain.reshape(1, D_MODEL), wg, wu, wd]
    out_shape = [jax.ShapeDtypeStruct((rows, D_MODEL), F32)]
    out_specs = [pl.BlockSpec((tm, D_MODEL), lambda i, j: (i, 0))]
    if emit_norm:
        in_specs.append(pl.BlockSpec((1, D_MODEL), lambda i, j: (0, 0)))
        args.append(next_gain.reshape(1, D_MODEL))
        out_shape.append(jax.ShapeDtypeStruct((rows, D_MODEL), BF16))
        out_specs.append(pl.BlockSpec((tm, D_MODEL), lambda i, j: (i, 0)))
    res = pl.pallas_call(
        functools.partial(_ffn_kernel, meta_row=meta_row, emit_norm=emit_norm),
        grid=(ni, nj),
        in_specs=in_specs,
        out_specs=out_specs,
        out_shape=out_shape,
        scratch_shapes=[pltpu.VMEM((tm, D_MODEL), BF16)],
        compiler_params=_params(2),
        name="ffn_norm" if emit_norm else "ffn",
    )(*args)
    return res if emit_norm else res[0]


def _proj_kernel(n_ref, w_ref, *refs, epilogue):
    p_ref = refs[0] if len(refs) == 3 else None
    o_ref, wb_ref = refs[-2:]

    @pl.when(pl.program_id(1) == 0)
    def _():
        wb_ref[...] = w_ref[...].T.astype(BF16)

    y = _mm(n_ref[...], wb_ref[...])
    epilogue(y, p_ref, o_ref, pl.program_id(0))


def _fox_epilogue(y, p_ref, o_ref, j):
    @pl.when(j < 2)
    def _():
        for h in range(y.shape[1] // HEAD_DIM):
            sl = slice(h * HEAD_DIM, (h + 1) * HEAD_DIM)
            o_ref[:, sl] = _rmsnorm_rows(y[:, sl], p_ref[:, sl]).astype(o_ref.dtype)

    @pl.when(j >= 2)
    def _():
        o_ref[...] = y.astype(o_ref.dtype)


def _scale_epilogue(y, p_ref, o_ref, j):
    o_ref[...] = (y * p_ref[...]).astype(o_ref.dtype)


def _gate_epilogue(y, p_ref, o_ref, j):
    o_ref[...] = jax.nn.sigmoid(y).astype(o_ref.dtype)


def _logf_epilogue(y, p_ref, o_ref, j):
    o_ref[...] = (_log_sigmoid(y + p_ref[...]) * LOG2E).astype(o_ref.dtype)


def _proj(n, wt, row0, n_cols, p, *, rows, tm, tn, epilogue, out_dtype, name):
    assert n_cols % tn == 0
    nj, ni = n_cols // tn, rows // tm
    if row0 % tn == 0:
        w_spec = pl.BlockSpec((tn, D_MODEL), lambda j, i: (row0 // tn + j, 0))
    else:
        assert row0 % SUBLANES == 0 and tn % SUBLANES == 0
        w_spec = pl.BlockSpec((pl.Element(tn), pl.Element(D_MODEL)),
                              lambda j, i: ((row0 // SUBLANES + j * (tn // SUBLANES)) * SUBLANES, 0))
    in_specs = [pl.BlockSpec((tm, D_MODEL), lambda j, i: (i, 0)), w_spec]
    args = [n, wt]
    if p is not None:
        in_specs.append(pl.BlockSpec((1, tn), lambda j, i: (0, j)))
        args.append(p)
    return pl.pallas_call(
        functools.partial(_proj_kernel, epilogue=epilogue),
        grid=(nj, ni),
        in_specs=in_specs,
        out_specs=pl.BlockSpec((tm, tn), lambda j, i: (i, j)),
        out_shape=jax.ShapeDtypeStruct((rows, n_cols), out_dtype),
        scratch_shapes=[pltpu.VMEM((D_MODEL, tn), BF16)],
        compiler_params=_params(2),
        name=name,
    )(*args)


def _decay_kernel(lf_ref, qc_ref, kc_ref, c_ref, *, n_batch, seq):
    T = DECAY_TERMS
    n_real = n_batch * seq
    r_i = lax.broadcasted_iota(jnp.int32, (LANES, LANES), 0)
    c_i = lax.broadcasted_iota(jnp.int32, (LANES, LANES), 1)
    tri = jnp.where(r_i >= c_i, 1.0, 0.0).astype(BF16)

    def cum_block(x, carry):
        acc = carry
        for p in _split_bf16(x, T):
            acc = acc + _mm(tri, p)
        return acc

    xm = jnp.concatenate([lf_ref[n_real:n_real + N_META, :], jnp.zeros((LANES - N_META, LANES), F32)], axis=0)
    cm = cum_block(xm, jnp.zeros((1, LANES), F32))
    c_ref[n_real:n_real + N_META, :] = cm[:N_META, :]
    base = cm[N_META - 1:N_META, :]

    for b in range(n_batch):
        def body(t, carry, b=b):
            r0 = pl.multiple_of(b * seq + t * LANES, LANES)
            c = cum_block(lf_ref[pl.ds(r0, LANES), :], carry)
            c_ref[pl.ds(r0, LANES), :] = c
            return c[LANES - 1:LANES, :]
        lax.fori_loop(0, seq // LANES, body, base)

    k_i = lax.broadcasted_iota(jnp.int32, (T * LANES, LANES), 0)
    n_i = lax.broadcasted_iota(jnp.int32, (T * LANES, LANES), 1)
    sel_q = jnp.zeros((T * LANES, LANES), F32)
    sel_k = jnp.zeros((T * LANES, LANES), F32)
    lane = lax.broadcasted_iota(jnp.int32, (1, LANES), 1)
    ones_q = jnp.zeros((1, LANES), F32)
    ones_k = jnp.zeros((1, LANES), F32)
    for h in range(N_HEADS):
        for p in range(T):
            row = p * LANES + h
            sel_q = jnp.where((k_i == row) & (n_i == 2 * T * h + p), 1.0, sel_q)
            sel_k = jnp.where((k_i == row) & (n_i == 2 * T * h + T + p), -1.0, sel_k)
            ones_q = jnp.where(lane == 2 * T * h + T + p, 1.0, ones_q)
            ones_k = jnp.where(lane == 2 * T * h + p, 1.0, ones_k)
    sel_q = sel_q.astype(BF16)
    sel_k = sel_k.astype(BF16)

    def expand(c):
        pieces = jnp.concatenate(_split_bf16(c, T), axis=1)
        return ((_mm(pieces, sel_q) + ones_q).astype(BF16), (_mm(pieces, sel_k) + ones_k).astype(BF16))

    chunk = 1024
    def ebody(t, _):
        r0 = pl.multiple_of(t * chunk, chunk)
        q, k = expand(c_ref[pl.ds(r0, chunk), :])
        qc_ref[pl.ds(r0, chunk), :] = q
        kc_ref[pl.ds(r0, chunk), :] = k
        return 0
    lax.fori_loop(0, n_real // chunk, ebody, 0)
    _, km = expand(c_ref[n_real:n_real + N_META, :])
    kc_ref[n_real:n_real + N_META, :] = km


def _decay(lf, *, n_batch, seq):
    n_real = n_batch * seq
    return pl.pallas_call(
        functools.partial(_decay_kernel, n_batch=n_batch, seq=seq),
        out_shape=[jax.ShapeDtypeStruct((n_real, LANES), BF16),
                   jax.ShapeDtypeStruct((n_real + N_META, LANES), BF16)],
        scratch_shapes=[pltpu.VMEM((n_real + N_META, LANES), F32)],
        compiler_params=pltpu.CompilerParams(vmem_limit_bytes=VMEM_LIMIT),
        name="decay",
    )(lf)


def _eye(n):
    return jnp.where(lax.broadcasted_iota(jnp.int32, (n, n), 0) == lax.broadcasted_iota(jnp.int32, (n, n), 1),
                     1.0, 0.0).astype(BF16)


def _stage_keys_values(kpad_ref, vt_ref, meta_k_refs, v_ref, mv_ref, *, seq):
    eye = _eye(HEAD_DIM)
    kpad_ref[...] = jnp.zeros_like(kpad_ref)
    for n, ref in enumerate(meta_k_refs):
        kpad_ref[0:N_META, n * HEAD_DIM:(n + 1) * HEAD_DIM] = ref[...]
    vpad = jnp.concatenate([mv_ref[...], jnp.zeros((LANES - N_META, HEAD_DIM), BF16)], axis=0)
    vt_ref[:, seq:seq + LANES] = _mm_nt(eye, vpad).astype(BF16)
    chunk = 512

    def body(c, _):
        c0 = pl.multiple_of(c * chunk, chunk)
        vt_ref[:, pl.ds(c0, chunk)] = _mm_nt(eye, v_ref[pl.ds(c0, chunk), :]).astype(BF16)
        return 0
    lax.fori_loop(0, seq // chunk, body, 0)


def _store_transposed(o_ref, r0, acc_t):
    o_ref[pl.ds(r0, ATT_SUB), :] = acc_t.T.astype(o_ref.dtype)


def _fox_kernel(q_ref, qc_ref, k_ref, kc_ref, v_ref, mk_ref, mkc_ref, mv_ref, o_ref, kpad_ref, vt_ref,
                sa_ref, sb_ref, *, seq):
    T = DECAY_TERMS
    h = pl.program_id(1)
    _stage_keys_values(kpad_ref, vt_ref, (mk_ref, mkc_ref), v_ref, mv_ref, seq=seq)

    lane = lax.broadcasted_iota(jnp.int32, (1, LANES), 1)
    head_lanes = (lane >= 2 * T * h) & (lane < 2 * T * (h + 1))
    meta_valid = lax.broadcasted_iota(jnp.int32, (LANES, ATT_SUB), 0) < N_META
    causal = (lax.broadcasted_iota(jnp.int32, (ATT_SUB, ATT_SUB), 0)
              <= lax.broadcasted_iota(jnp.int32, (ATT_SUB, ATT_SUB), 1))
    all_subs = tuple(range(ATT_NSUB))

    def keys(c0):
        return jnp.concatenate([k_ref[pl.ds(c0, ATT_SUB), :], kc_ref[pl.ds(c0, ATT_SUB), :]], axis=1)

    def online(s, vt, m, l, acc):
        m_new = jnp.maximum(m, jnp.max(s, axis=0, keepdims=True))
        alpha = jnp.exp2(m - m_new)
        p = jnp.exp2(s - m_new)
        l = alpha * l + jnp.sum(p, axis=0, keepdims=True)
        acc = alpha * acc + _mm(vt, p.astype(BF16))
        return m_new, l, acc

    def q_block(i, _):
        r0 = pl.multiple_of(i * ATT_TQ, ATT_TQ)
        qx = jnp.where(head_lanes, qc_ref[pl.ds(r0, ATT_TQ), :], jnp.zeros((), BF16))
        qa = jnp.concatenate([q_ref[pl.ds(r0, ATT_TQ), :], qx], axis=1)
        subs = [qa[n * ATT_SUB:(n + 1) * ATT_SUB] for n in all_subs]

        def issue(buf_ref, c0, which=all_subs):
            ka = keys(c0)
            for n in which:
                buf_ref[n] = _mm_nt(ka, subs[n])

        def consume(buf_ref, c0, state, masked=None, which=all_subs):
            vt = vt_ref[:, pl.ds(c0, ATT_SUB)]
            out = list(state)
            for n in which:
                s = buf_ref[n]
                if n == masked:
                    s = jnp.where(causal, s, -jnp.inf)
                out[n] = online(s, vt, *out[n])
            return tuple(out)

        issue(sa_ref, 0)

        state = []
        for qs in subs:
            s = jnp.where(meta_valid, _mm_nt(kpad_ref[...], qs), -jnp.inf)
            m = jnp.max(s, axis=0, keepdims=True)
            p = jnp.exp2(s - m)
            state.append((m, jnp.sum(p, axis=0, keepdims=True), _mm(vt_ref[:, seq:seq + LANES], p.astype(BF16))))

        def kv_pair(t, state):
            c0 = pl.multiple_of(t * (2 * ATT_SUB), 2 * ATT_SUB)
            c1 = pl.multiple_of(c0 + ATT_SUB, ATT_SUB)
            issue(sb_ref, c1)
            state = consume(sa_ref, c0, state)
            issue(sa_ref, pl.multiple_of(c0 + 2 * ATT_SUB, 2 * ATT_SUB))
            return consume(sb_ref, c1, state)

        state = lax.fori_loop(0, i * (ATT_NSUB // 2), kv_pair, tuple(state))

        bufs = (sa_ref, sb_ref)
        for kb in all_subs:
            if kb + 1 < ATT_NSUB:
                issue(bufs[(kb + 1) % 2], pl.multiple_of(r0 + (kb + 1) * ATT_SUB, ATT_SUB), all_subs[kb + 1:])
            state = consume(bufs[kb % 2], pl.multiple_of(r0 + kb * ATT_SUB, ATT_SUB), state, kb, all_subs[kb:])
        for n, (m, l, acc) in enumerate(state):
            _store_transposed(o_ref, pl.multiple_of(r0 + n * ATT_SUB, ATT_SUB), acc / l)
        return 0

    lax.fori_loop(0, seq // ATT_TQ, q_block, 0)


def _sb_kernel(q_ref, k_ref, v_ref, mk_ref, mv_ref, o_ref, kpad_ref, vt_ref, z_ref, *, seq):
    _stage_keys_values(kpad_ref, vt_ref, (mk_ref,), v_ref, mv_ref, seq=seq)

    def from_here_on(n):
        return jnp.where(lax.broadcasted_iota(jnp.int32, (n, n), 0) <= lax.broadcasted_iota(jnp.int32, (n, n), 1),
                         1.0, 0.0).astype(BF16)

    u_blk = from_here_on(ATT_SUB)
    u_meta = from_here_on(LANES)
    meta_valid = lax.broadcasted_iota(jnp.int32, (LANES, ATT_SUB), 0) < N_META
    strict = (lax.broadcasted_iota(jnp.int32, (ATT_SUB, ATT_SUB), 0)
              < lax.broadcasted_iota(jnp.int32, (ATT_SUB, ATT_SUB), 1))
    all_subs = tuple(range(ATT_NSUB))

    def steps(items, states, zs=None):
        states = list(states)
        if zs is None:
            zs = [_mm_nt(kblk, qh) for _, qh, kblk, _, _, _ in items]
        cums = []
        for z, (_, _, _, _, u, mask) in zip(zs, items):
            sp = jnp.maximum(jnp.log2(1.0 + jnp.exp2(jnp.minimum(z, MAX_EXP2))), z)
            if mask is not None:
                sp = jnp.where(mask, sp, 0.0)
            hi, lo = _split_bf16(sp, 2)
            cums.append(_mm(u, hi) + _mm(u, lo))
        for z, cum, (half, _, _, vt, _, mask) in zip(zs, cums, items):
            r, acc = states[half]
            w = jnp.exp2(z - cum - r)
            if mask is not None:
                w = jnp.where(mask, w, 0.0)
            states[half] = (r + cum[0:1, :], acc + _mm(vt, w.astype(BF16)))
        return tuple(states)

    def q_block(i, _):
        r0 = pl.multiple_of(i * ATT_TQ, ATT_TQ)
        q = [q_ref[pl.ds(pl.multiple_of(r0 + n * ATT_SUB, ATT_SUB), ATT_SUB), :] for n in all_subs]
        zero = (jnp.zeros((1, ATT_SUB), F32), jnp.zeros((HEAD_DIM, ATT_SUB), F32))

        def item(n, c0, mask):
            return (n, q[n], k_ref[pl.ds(c0, ATT_SUB), :], vt_ref[:, pl.ds(c0, ATT_SUB)], u_blk, mask)

        def pair_items(jj):
            items = []
            for d in (1, 2):
                c0 = pl.multiple_of(jnp.maximum(r0 - (2 * jj + d) * ATT_SUB, 0), ATT_SUB)
                items += [item(n, c0, None) for n in all_subs]
            return items

        def scores(items):
            return [_mm_nt(kblk, qs) for _, qs, kblk, _, _, _ in items]

        for n, z in enumerate(scores(pair_items(0))):
            z_ref[n] = z

        diag = []
        for kb in reversed(all_subs):
            c0 = pl.multiple_of(r0 + kb * ATT_SUB, ATT_SUB)
            diag += [item(n, c0, strict if n == kb else None) for n in all_subs[kb:]]
        states = steps(diag, (zero,) * ATT_NSUB)

        def kv_pair(jj, states):
            zs = [z_ref[n] for n in range(2 * ATT_NSUB)]
            ahead = scores(pair_items(jj + 1))
            states = steps(pair_items(jj), states, zs)
            for n, z in enumerate(ahead):
                z_ref[n] = z
            return states

        states = lax.fori_loop(0, i * (ATT_NSUB // 2), kv_pair, states)
        meta = [(n, q[n], kpad_ref[...], vt_ref[:, seq:seq + LANES], u_meta, meta_valid) for n in all_subs]
        states = steps(meta, states)
        for n, (_, acc) in enumerate(states):
            _store_transposed(o_ref, pl.multiple_of(r0 + n * ATT_SUB, ATT_SUB), acc)
        return 0

    lax.fori_loop(0, seq // ATT_TQ, q_block, 0)


def _head_spec(rows, col_block0):
    return pl.BlockSpec((rows, HEAD_DIM), lambda b, h: (b, col_block0 + h))


def _meta_spec(n_real, col_block0):
    return pl.BlockSpec((N_META, HEAD_DIM), lambda b, h: (n_real // N_META, col_block0 + h))


def _fox_attention(qkv, qc, kc, *, n_batch, seq):
    n_real = n_batch * seq
    shared = pl.BlockSpec((seq, LANES), lambda b, h: (b, 0))
    return pl.pallas_call(
        functools.partial(_fox_kernel, seq=seq),
        grid=(n_batch, N_HEADS),
        in_specs=[_head_spec(seq, 0), shared, _head_spec(seq, N_HEADS), shared, _head_spec(seq, 2 * N_HEADS),
                  _meta_spec(n_real, N_HEADS), pl.BlockSpec((N_META, LANES), lambda b, h: (n_real // N_META, 0)),
                  _meta_spec(n_real, 2 * N_HEADS)],
        out_specs=_head_spec(seq, 0),
        out_shape=jax.ShapeDtypeStruct((n_real, WIDTH), BF16),
        scratch_shapes=[pltpu.VMEM((LANES, 2 * HEAD_DIM), BF16), pltpu.VMEM((HEAD_DIM, seq + LANES), BF16),
                        pltpu.VMEM((ATT_NSUB, ATT_SUB, ATT_SUB), F32), pltpu.VMEM((ATT_NSUB, ATT_SUB, ATT_SUB), F32)],
        compiler_params=_params(2),
        name="fox_attention",
    )(qkv, qc, qkv, kc, qkv, qkv, kc, qkv)


def _sb_attention(qkv, *, n_batch, seq):
    n_real = n_batch * seq
    return pl.pallas_call(
        functools.partial(_sb_kernel, seq=seq),
        grid=(n_batch, N_HEADS),
        in_specs=[_head_spec(seq, 0), _head_spec(seq, N_HEADS), _head_spec(seq, 2 * N_HEADS),
                  _meta_spec(n_real, N_HEADS), _meta_spec(n_real, 2 * N_HEADS)],
        out_specs=_head_spec(seq, 0),
        out_shape=jax.ShapeDtypeStruct((n_real, WIDTH), BF16),
        scratch_shapes=[pltpu.VMEM((LANES, HEAD_DIM), BF16), pltpu.VMEM((HEAD_DIM, seq + LANES), BF16),
                        pltpu.VMEM((2 * ATT_NSUB, ATT_SUB, ATT_SUB), F32)],
        compiler_params=_params(2),
        name="sb_attention",
    )(qkv, qkv, qkv, qkv, qkv)


def _mix_out_kernel(h_ref, of_ref, os_ref, gf_ref, gs_ref, wbf_ref, wbs_ref, wo_ref, o_ref):
    @pl.when(pl.program_id(1) == 0)
    def _():
        o_ref[...] = h_ref[...]

    tf = _mm(of_ref[...], wbf_ref[...].astype(BF16))
    ts = _mm(os_ref[...], wbs_ref[...].astype(BF16))
    merged = gf_ref[...].astype(F32) * tf + gs_ref[...].astype(F32) * ts
    o_ref[...] += _mm(merged.astype(BF16), wo_ref[...].astype(BF16))


def _mix_out(h, o_fox, o_sb, gates, wbf, wbs, wo, *, rows, tm, tn):
    ni, nj = rows // tm, D_MODEL // tn
    return pl.pallas_call(
        _mix_out_kernel,
        grid=(ni, nj),
        in_specs=[
            pl.BlockSpec((tm, D_MODEL), lambda i, j: (i, 0)),
            pl.BlockSpec((tm, WIDTH), lambda i, j: (i, 0)),
            pl.BlockSpec((tm, WIDTH), lambda i, j: (i, 0)),
            pl.BlockSpec((tm, tn), lambda i, j: (i, j)),
            pl.BlockSpec((tm, tn), lambda i, j: (i, nj + j)),
            pl.BlockSpec((WIDTH, tn), lambda i, j: (0, j)),
            pl.BlockSpec((WIDTH, tn), lambda i, j: (0, j)),
            pl.BlockSpec((tn, D_MODEL), lambda i, j: (j, 0)),
        ],
        out_specs=pl.BlockSpec((tm, D_MODEL), lambda i, j: (i, 0)),
        out_shape=jax.ShapeDtypeStruct((rows, D_MODEL), F32),
        compiler_params=_params(2),
        name="mix_out",
    )(h, o_fox, o_sb, gates, gates, wbf, wbs, wo)


def kernel(x, meta_tokens, ffn1_norm, ffn1_w_gate, ffn1_w_up, ffn1_w_down, mix_norm, w_in, b_forget, fox_q_norm, fox_k_norm, w_branch_fox, w_branch_sb, w_out, ffn2_norm, ffn2_w_gate, ffn2_w_up, ffn2_w_down):
    n_batch, seq, _ = x.shape
    n_real = n_batch * seq
    n_all = n_real + N_META
    assert ffn1_norm.shape[0] == 1
    assert n_all % ROW_TILE_ALL == 0 and n_real % ROW_TILE_REAL == 0 and seq % ATT_TQ == 0

    wt = w_in[0].T
    r_f = 3 * WIDTH
    r_sb = r_f + N_HEADS
    r_gate = r_sb + 3 * WIDTH
    wt_f = jnp.pad(wt[r_f:r_sb], ((0, LANES - N_HEADS), (0, 0)))
    b_f = jnp.pad(b_forget[0].reshape(1, N_HEADS), ((0, 0), (0, LANES - N_HEADS)))
    fox_gain = jnp.concatenate([fox_q_norm[0].reshape(1, WIDTH) * (SCALE * LOG2E), fox_k_norm[0].reshape(1, WIDTH),
                                jnp.ones((1, WIDTH), F32)], axis=1)
    sb_scale = jnp.concatenate([jnp.full((1, WIDTH), SCALE * LOG2E, F32), jnp.ones((1, 2 * WIDTH), F32)], axis=1)

    h1, n1 = _ffn(x.reshape(n_real, D_MODEL), meta_tokens.astype(F32), ffn1_norm[0], ffn1_w_gate[0],
                  ffn1_w_up[0], ffn1_w_down[0], rows=n_all, tm=ROW_TILE_ALL, next_gain=mix_norm[0])

    proj = functools.partial(_proj, n1, rows=n_all, tm=ROW_TILE_ALL)
    fox_qkv = proj(wt, 0, 3 * WIDTH, fox_gain, tn=PROJ_TILE_N, epilogue=_fox_epilogue,
                   out_dtype=BF16, name="proj_fox")
    log_f = proj(wt_f, 0, LANES, b_f, tn=LANES, epilogue=_logf_epilogue, out_dtype=F32, name="proj_logf")
    sb_qkv = proj(wt, r_sb, 3 * WIDTH, sb_scale, tn=PROJ_TILE_N, epilogue=_scale_epilogue,
                  out_dtype=BF16, name="proj_sb")
    gates = _proj(n1, wt, r_gate, 2 * D_MODEL, None, rows=n_real, tm=ROW_TILE_REAL,
                  tn=PROJ_TILE_N, epilogue=_gate_epilogue, out_dtype=BF16, name="proj_gates")

    qc, kc = _decay(log_f, n_batch=n_batch, seq=seq)
    o_fox = _fox_attention(fox_qkv, qc, kc, n_batch=n_batch, seq=seq)
    o_sb = _sb_attention(sb_qkv, n_batch=n_batch, seq=seq)

    h2 = _mix_out(h1, o_fox, o_sb, gates, w_branch_fox[0], w_branch_sb[0], w_out[0],
                  rows=n_real, tm=ROW_TILE_REAL, tn=OUT_TILE_N)
    h3 = _ffn(h2, None, ffn2_norm[0], ffn2_w_gate[0], ffn2_w_up[0], ffn2_w_down[0],
              rows=n_real, tm=ROW_TILE_REAL)
    return h3.reshape(n_batch, seq, D_MODEL)
```

```python
import functools
import math

import jax
import jax.numpy as jnp
from jax import lax
from jax.experimental import pallas as pl
from jax.experimental.pallas import tpu as pltpu

F32 = jnp.float32
BF16 = jnp.bfloat16

D_MODEL = 2048
D_FF = 5632
N_META = 16
HEAD_DIM = 128
N_HEADS = 8
WIDTH = N_HEADS * HEAD_DIM
RMS_EPS = 1e-6
FFN_RESIDUAL_WEIGHT = 0.5
SCALE = HEAD_DIM ** -0.5
LOG2E = math.log2(math.e)

LANES = 128
SUBLANES = 8
BF16_ROWS = 16
VMEM_LIMIT = 58 * 2**20

ROW_TILE_ALL = 912
ROW_TILE_REAL = 1024
FFN_TILE_F = 256
PROJ_TILE_N = 1024
OUT_TILE_N = 256
ATT_SUB = 256
ATT_NSUB = 4
ATT_TQ = ATT_SUB * ATT_NSUB
DECAY_TERMS = 3
SUFFIX_TERMS = 1
MAX_EXP2 = 126.0


def _mm(a, b):
    return jnp.dot(a, b, preferred_element_type=F32)


def _mm_nt(a, b):
    return lax.dot_general(a, b, (((1,), (1,)), ((), ())), preferred_element_type=F32)


def _split_bf16(x, n):
    parts = [x.astype(BF16)]
    for _ in range(n - 1):
        x = x - parts[-1].astype(F32)
        parts.append(x.astype(BF16))
    return parts


def _rmsnorm_rows(h, gain):
    ms = jnp.mean(h * h, axis=-1, keepdims=True)
    return h * lax.rsqrt(ms + RMS_EPS) * gain


def _log_sigmoid(z):
    return jnp.minimum(z, 0.0) - jnp.log(1.0 + jnp.exp(-jnp.abs(z)))


def _params(n_grid_axes):
    return pltpu.CompilerParams(dimension_semantics=("arbitrary",) * n_grid_axes,
                                vmem_limit_bytes=VMEM_LIMIT)


def _ffn_kernel(*refs, meta_row, emit_norm):
    refs = list(refs)
    x_ref = refs.pop(0)
    meta_ref = refs.pop(0) if meta_row is not None else None
    gain_ref, wg_ref, wu_ref, wd_ref = refs[:4]
    refs = refs[4:]
    if emit_norm:
        ngain_ref, o_ref, n_ref, xn_ref = refs
    else:
        o_ref, xn_ref = refs
    i = pl.program_id(0)
    j = pl.program_id(1)

    @pl.when(j == 0)
    def _():
        if meta_row is None:
            o_ref[...] = x_ref[...]
        else:
            last = pl.num_programs(0) - 1

            @pl.when(i < last)
            def _():
                o_ref[...] = x_ref[...]

            @pl.when(i == last)
            def _():
                o_ref[0:meta_row, :] = x_ref[0:meta_row, :]
                o_ref[meta_row:meta_row + N_META, :] = meta_ref[...]
        xn_ref[...] = _rmsnorm_rows(o_ref[...], gain_ref[...]).astype(BF16)

    xn = xn_ref[...]
    g = _mm(xn, wg_ref[...].astype(BF16))
    u = _mm(xn, wu_ref[...].astype(BF16))
    a = (g * jax.nn.sigmoid(g)) * (u * FFN_RESIDUAL_WEIGHT)
    o_ref[...] += _mm(a.astype(BF16), wd_ref[...].astype(BF16))

    if emit_norm:
        @pl.when(j == pl.num_programs(1) - 1)
        def _():
            n_ref[...] = _rmsnorm_rows(o_ref[...], ngain_ref[...]).astype(BF16)


def _ffn(x, meta, gain, wg, wu, wd, *, rows, tm, next_gain=None):
    ni = rows // tm
    nj = D_FF // FFN_TILE_F
    emit_norm = next_gain is not None
    in_specs = [pl.BlockSpec((tm, D_MODEL), lambda i, j: (i, 0))]
    args = [x]
    meta_row = None
    if meta is not None:
        meta_row = x.shape[0] - (ni - 1) * tm
        assert meta_row + N_META == tm
        in_specs.append(pl.BlockSpec((N_META, D_MODEL), lambda i, j: (0, 0)))
        args.append(meta)
    in_specs += [
        pl.BlockSpec((1, D_MODEL), lambda i, j: (0, 0)),
        pl.BlockSpec((D_MODEL, FFN_TILE_F), lambda i, j: (0, j)),
        pl.BlockSpec((D_MODEL, FFN_TILE_F), lambda i, j: (0, j)),
        pl.BlockSpec((FFN_TILE_F, D_MODEL), lambda i, j: (j, 0)),
    ]
    args += [gain.reshape(1, D_MODEL), wg, wu, wd]
    out_shape = [jax.ShapeDtypeStruct((rows, D_MODEL), F32)]
    out_specs = [pl.BlockSpec((tm, D_MODEL), lambda i, j: (i, 0))]
    if emit_norm:
        in_specs.append(pl.BlockSpec((1, D_MODEL), lambda i, j: (0, 0)))
        args.append(next_gain.reshape(1, D_MODEL))
        out_shape.append(jax.ShapeDtypeStruct((rows, D_MODEL), BF16))
        out_specs.append(pl.BlockSpec((tm, D_MODEL), lambda i, j: (i, 0)))
    res = pl.pallas_call(
        functools.partial(_ffn_kernel, meta_row=meta_row, emit_norm=emit_norm),
        grid=(ni, nj),
        in_specs=in_specs,
        out_specs=out_specs,
        out_shape=out_shape,
        scratch_shapes=[pltpu.VMEM((tm, D_MODEL), BF16)],
        compiler_params=_params(2),
        name="ffn_norm" if emit_norm else "ffn",
    )(*args)
    return res if emit_norm else res[0]


def _proj_kernel(n_ref, w_ref, *refs, epilogue):
    p_ref = refs[0] if len(refs) == 3 else None
    o_ref, wb_ref = refs[-2:]

    @pl.when(pl.program_id(1) == 0)
    def _():
        wb_ref[...] = w_ref[...].T.astype(BF16)

    for rows in _row_chunks(n_ref.shape[0]):
        y = _mm(n_ref[rows, :], wb_ref[...])
        epilogue(y, p_ref, o_ref, rows, pl.program_id(0))


def _row_chunks(tm):
    n = next(n for n in (4, 3, 2, 1) if tm % (n * BF16_ROWS) == 0)
    return [slice(c * (tm // n), (c + 1) * (tm // n)) for c in range(n)]


def _fox_epilogue(y, p_ref, o_ref, rows, j):
    @pl.when(j < 2)
    def _():
        for h in range(y.shape[1] // HEAD_DIM):
            sl = slice(h * HEAD_DIM, (h + 1) * HEAD_DIM)
            o_ref[rows, sl] = _rmsnorm_rows(y[:, sl], p_ref[:, sl]).astype(o_ref.dtype)

    @pl.when(j >= 2)
    def _():
        o_ref[rows, :] = y.astype(o_ref.dtype)


def _scale_epilogue(y, p_ref, o_ref, rows, j):
    o_ref[rows, :] = (y * p_ref[...]).astype(o_ref.dtype)


def _gate_epilogue(y, p_ref, o_ref, rows, j):
    o_ref[rows, :] = jax.nn.sigmoid(y).astype(o_ref.dtype)


def _logf_epilogue(y, p_ref, o_ref, rows, j):
    o_ref[rows, :] = (_log_sigmoid(y + p_ref[...]) * LOG2E).astype(o_ref.dtype)


def _proj(n, wt, row0, n_cols, p, *, rows, tm, tn, epilogue, out_dtype, name):
    assert n_cols % tn == 0
    nj, ni = n_cols // tn, rows // tm
    if row0 % tn == 0:
        w_spec = pl.BlockSpec((tn, D_MODEL), lambda j, i: (row0 // tn + j, 0))
    else:
        assert row0 % SUBLANES == 0 and tn % SUBLANES == 0
        w_spec = pl.BlockSpec((pl.Element(tn), pl.Element(D_MODEL)),
                              lambda j, i: ((row0 // SUBLANES + j * (tn // SUBLANES)) * SUBLANES, 0))
    in_specs = [pl.BlockSpec((tm, D_MODEL), lambda j, i: (i, 0)), w_spec]
    args = [n, wt]
    if p is not None:
        in_specs.append(pl.BlockSpec((1, tn), lambda j, i: (0, j)))
        args.append(p)
    return pl.pallas_call(
        functools.partial(_proj_kernel, epilogue=epilogue),
        grid=(nj, ni),
        in_specs=in_specs,
        out_specs=pl.BlockSpec((tm, tn), lambda j, i: (i, j)),
        out_shape=jax.ShapeDtypeStruct((rows, n_cols), out_dtype),
        scratch_shapes=[pltpu.VMEM((D_MODEL, tn), BF16)],
        compiler_params=_params(2),
        name=name,
    )(*args)


def _decay_kernel(lf_ref, qc_ref, kc_ref, c_ref, *, n_batch, seq):
    T = DECAY_TERMS
    n_real = n_batch * seq
    r_i = lax.broadcasted_iota(jnp.int32, (LANES, LANES), 0)
    c_i = lax.broadcasted_iota(jnp.int32, (LANES, LANES), 1)
    tri = jnp.where(r_i >= c_i, 1.0, 0.0).astype(BF16)

    def cum_block(x, carry):
        acc = carry
        for p in _split_bf16(x, T):
            acc = acc + _mm(tri, p)
        return acc

    xm = jnp.concatenate([lf_ref[n_real:n_real + N_META, :], jnp.zeros((LANES - N_META, LANES), F32)], axis=0)
    cm = cum_block(xm, jnp.zeros((1, LANES), F32))
    c_ref[n_real:n_real + N_META, :] = cm[:N_META, :]
    base = cm[N_META - 1:N_META, :]

    for b in range(n_batch):
        def body(t, carry, b=b):
            r0 = pl.multiple_of(b * seq + t * LANES, LANES)
            c = cum_block(lf_ref[pl.ds(r0, LANES), :], carry)
            c_ref[pl.ds(r0, LANES), :] = c
            return c[LANES - 1:LANES, :]
        lax.fori_loop(0, seq // LANES, body, base)

    k_i = lax.broadcasted_iota(jnp.int32, (T * LANES, LANES), 0)
    n_i = lax.broadcasted_iota(jnp.int32, (T * LANES, LANES), 1)
    sel_q = jnp.zeros((T * LANES, LANES), F32)
    sel_k = jnp.zeros((T * LANES, LANES), F32)
    lane = lax.broadcasted_iota(jnp.int32, (1, LANES), 1)
    ones_q = jnp.zeros((1, LANES), F32)
    ones_k = jnp.zeros((1, LANES), F32)
    for h in range(N_HEADS):
        for p in range(T):
            row = p * LANES + h
            sel_q = jnp.where((k_i == row) & (n_i == 2 * T * h + p), 1.0, sel_q)
            sel_k = jnp.where((k_i == row) & (n_i == 2 * T * h + T + p), -1.0, sel_k)
            ones_q = jnp.where(lane == 2 * T * h + T + p, 1.0, ones_q)
            ones_k = jnp.where(lane == 2 * T * h + p, 1.0, ones_k)
    sel_q = sel_q.astype(BF16)
    sel_k = sel_k.astype(BF16)

    def expand(c):
        pieces = jnp.concatenate(_split_bf16(c, T), axis=1)
        return ((_mm(pieces, sel_q) + ones_q).astype(BF16), (_mm(pieces, sel_k) + ones_k).astype(BF16))

    chunk = 1024
    def ebody(t, _):
        r0 = pl.multiple_of(t * chunk, chunk)
        q, k = expand(c_ref[pl.ds(r0, chunk), :])
        qc_ref[pl.ds(r0, chunk), :] = q
        kc_ref[pl.ds(r0, chunk), :] = k
        return 0
    lax.fori_loop(0, n_real // chunk, ebody, 0)
    _, km = expand(c_ref[n_real:n_real + N_META, :])
    kc_ref[n_real:n_real + N_META, :] = km


def _decay(lf, *, n_batch, seq):
    n_real = n_batch * seq
    return pl.pallas_call(
        functools.partial(_decay_kernel, n_batch=n_batch, seq=seq),
        out_shape=[jax.ShapeDtypeStruct((n_real, LANES), BF16),
                   jax.ShapeDtypeStruct((n_real + N_META, LANES), BF16)],
        scratch_shapes=[pltpu.VMEM((n_real + N_META, LANES), F32)],
        compiler_params=pltpu.CompilerParams(vmem_limit_bytes=VMEM_LIMIT),
        name="decay",
    )(lf)


def _eye(n):
    return jnp.where(lax.broadcasted_iota(jnp.int32, (n, n), 0) == lax.broadcasted_iota(jnp.int32, (n, n), 1),
                     1.0, 0.0).astype(BF16)


def _stage_keys_values(kpad_ref, vt_ref, meta_k_refs, v_ref, mv_ref, *, seq):
    eye = _eye(HEAD_DIM)
    kpad_ref[...] = jnp.zeros_like(kpad_ref)
    for n, ref in enumerate(meta_k_refs):
        kpad_ref[0:N_META, n * HEAD_DIM:(n + 1) * HEAD_DIM] = ref[...]
    vpad = jnp.concatenate([mv_ref[...], jnp.zeros((LANES - N_META, HEAD_DIM), BF16)], axis=0)
    vt_ref[:, seq:seq + LANES] = _mm_nt(eye, vpad).astype(BF16)
    chunk = 512

    def body(c, _):
        c0 = pl.multiple_of(c * chunk, chunk)
        vt_ref[:, pl.ds(c0, chunk)] = _mm_nt(eye, v_ref[pl.ds(c0, chunk), :]).astype(BF16)
        return 0
    lax.fori_loop(0, seq // chunk, body, 0)


def _store_transposed(o_ref, r0, acc_t):
    o_ref[pl.ds(r0, ATT_SUB), :] = acc_t.T.astype(o_ref.dtype)


def _fox_kernel(q_ref, qc_ref, k_ref, kc_ref, v_ref, mk_ref, mkc_ref, mv_ref, o_ref, kpad_ref, vt_ref,
                sa_ref, sb_ref, *, seq):
    T = DECAY_TERMS
    h = pl.program_id(1)
    _stage_keys_values(kpad_ref, vt_ref, (mk_ref, mkc_ref), v_ref, mv_ref, seq=seq)

    lane = lax.broadcasted_iota(jnp.int32, (1, LANES), 1)
    head_lanes = (lane >= 2 * T * h) & (lane < 2 * T * (h + 1))
    meta_valid = lax.broadcasted_iota(jnp.int32, (LANES, ATT_SUB), 0) < N_META
    causal = (lax.broadcasted_iota(jnp.int32, (ATT_SUB, ATT_SUB), 0)
              <= lax.broadcasted_iota(jnp.int32, (ATT_SUB, ATT_SUB), 1))
    all_subs = tuple(range(ATT_NSUB))

    def keys(c0):
        return jnp.concatenate([k_ref[pl.ds(c0, ATT_SUB), :], kc_ref[pl.ds(c0, ATT_SUB), :]], axis=1)

    def online(s, vt, m, l, acc):
        m_new = jnp.maximum(m, jnp.max(s, axis=0, keepdims=True))
        alpha = jnp.exp2(m - m_new)
        p = jnp.exp2(s - m_new)
        l = alpha * l + jnp.sum(p, axis=0, keepdims=True)
        acc = alpha * acc + _mm(vt, p.astype(BF16))
        return m_new, l, acc

    def q_block(i, _):
        r0 = pl.multiple_of(i * ATT_TQ, ATT_TQ)
        qx = jnp.where(head_lanes, qc_ref[pl.ds(r0, ATT_TQ), :], jnp.zeros((), BF16))
        qa = jnp.concatenate([q_ref[pl.ds(r0, ATT_TQ), :], qx], axis=1)
        subs = [qa[n * ATT_SUB:(n + 1) * ATT_SUB] for n in all_subs]

        def issue(buf_ref, c0, which=all_subs):
            ka = keys(c0)
            for n in which:
                buf_ref[n] = _mm_nt(ka, subs[n])

        def consume(buf_ref, c0, state, masked=None, which=all_subs):
            vt = vt_ref[:, pl.ds(c0, ATT_SUB)]
            out = list(state)
            for n in which:
                s = buf_ref[n]
                if n == masked:
                    s = jnp.where(causal, s, -jnp.inf)
                out[n] = online(s, vt, *out[n])
            return tuple(out)

        issue(sa_ref, 0)

        state = []
        for qs in subs:
            s = jnp.where(meta_valid, _mm_nt(kpad_ref[...], qs), -jnp.inf)
            m = jnp.max(s, axis=0, keepdims=True)
            p = jnp.exp2(s - m)
            state.append((m, jnp.sum(p, axis=0, keepdims=True), _mm(vt_ref[:, seq:seq + LANES], p.astype(BF16))))

        def kv_pair(t, state):
            c0 = pl.multiple_of(t * (2 * ATT_SUB), 2 * ATT_SUB)
            c1 = pl.multiple_of(c0 + ATT_SUB, ATT_SUB)
            issue(sb_ref, c1)
            state = consume(sa_ref, c0, state)
            issue(sa_ref, pl.multiple_of(c0 + 2 * ATT_SUB, 2 * ATT_SUB))
            return consume(sb_ref, c1, state)

        state = lax.fori_loop(0, i * (ATT_NSUB // 2), kv_pair, tuple(state))

        bufs = (sa_ref, sb_ref)
        for kb in all_subs:
            if kb + 1 < ATT_NSUB:
                issue(bufs[(kb + 1) % 2], pl.multiple_of(r0 + (kb + 1) * ATT_SUB, ATT_SUB), all_subs[kb + 1:])
            state = consume(bufs[kb % 2], pl.multiple_of(r0 + kb * ATT_SUB, ATT_SUB), state, kb, all_subs[kb:])
        for n, (m, l, acc) in enumerate(state):
            _store_transposed(o_ref, pl.multiple_of(r0 + n * ATT_SUB, ATT_SUB), acc / l)
        return 0

    lax.fori_loop(0, seq // ATT_TQ, q_block, 0)


def _sb_kernel(q_ref, k_ref, v_ref, mk_ref, mv_ref, o_ref, kpad_ref, vt_ref, z_ref, *, seq):
    _stage_keys_values(kpad_ref, vt_ref, (mk_ref,), v_ref, mv_ref, seq=seq)

    def from_here_on(n):
        return jnp.where(lax.broadcasted_iota(jnp.int32, (n, n), 0) <= lax.broadcasted_iota(jnp.int32, (n, n), 1),
                         1.0, 0.0).astype(BF16)

    u_blk = from_here_on(ATT_SUB)
    u_meta = from_here_on(LANES)
    meta_valid = lax.broadcasted_iota(jnp.int32, (LANES, ATT_SUB), 0) < N_META
    strict = (lax.broadcasted_iota(jnp.int32, (ATT_SUB, ATT_SUB), 0)
              < lax.broadcasted_iota(jnp.int32, (ATT_SUB, ATT_SUB), 1))
    all_subs = tuple(range(ATT_NSUB))

    def steps(items, states, zs=None):
        states = list(states)
        if zs is None:
            zs = [_mm_nt(kblk, qh) for _, qh, kblk, _, _, _ in items]
        cums = []
        for z, (_, _, _, _, u, mask) in zip(zs, items):
            sp = jnp.maximum(jnp.log2(1.0 + jnp.exp2(jnp.minimum(z, MAX_EXP2))), z)
            if mask is not None:
                sp = jnp.where(mask, sp, 0.0)
            pieces = _split_bf16(sp, SUFFIX_TERMS)
            cums.append(functools.reduce(lambda a, b: a + b, [_mm(u, p) for p in pieces]))
        for z, cum, (half, _, _, vt, _, mask) in zip(zs, cums, items):
            r, acc = states[half]
            w = jnp.exp2(z - cum - r)
            if mask is not None:
                w = jnp.where(mask, w, 0.0)
            states[half] = (r + cum[0:1, :], acc + _mm(vt, w.astype(BF16)))
        return tuple(states)

    def q_block(i, _):
        r0 = pl.multiple_of(i * ATT_TQ, ATT_TQ)
        q = [q_ref[pl.ds(pl.multiple_of(r0 + n * ATT_SUB, ATT_SUB), ATT_SUB), :] for n in all_subs]
        zero = (jnp.zeros((1, ATT_SUB), F32), jnp.zeros((HEAD_DIM, ATT_SUB), F32))

        def item(n, c0, mask):
            return (n, q[n], k_ref[pl.ds(c0, ATT_SUB), :], vt_ref[:, pl.ds(c0, ATT_SUB)], u_blk, mask)

        def pair_items(jj):
            items = []
            for d in (1, 2):
                c0 = pl.multiple_of(jnp.maximum(r0 - (2 * jj + d) * ATT_SUB, 0), ATT_SUB)
                items += [item(n, c0, None) for n in all_subs]
            return items

        def scores(items):
            return [_mm_nt(kblk, qs) for _, qs, kblk, _, _, _ in items]

        for n, z in enumerate(scores(pair_items(0))):
            z_ref[n] = z

        diag = []
        for kb in reversed(all_subs):
            c0 = pl.multiple_of(r0 + kb * ATT_SUB, ATT_SUB)
            diag += [item(n, c0, strict if n == kb else None) for n in all_subs[kb:]]
        states = steps(diag, (zero,) * ATT_NSUB)

        def kv_pair(jj, states):
            zs = [z_ref[n] for n in range(2 * ATT_NSUB)]
            ahead = scores(pair_items(jj + 1))
            states = steps(pair_items(jj), states, zs)
            for n, z in enumerate(ahead):
                z_ref[n] = z
            return states

        states = lax.fori_loop(0, i * (ATT_NSUB // 2), kv_pair, states)
        meta = [(n, q[n], kpad_ref[...], vt_ref[:, seq:seq + LANES], u_meta, meta_valid) for n in all_subs]
        states = steps(meta, states)
        for n, (_, acc) in enumerate(states):
            _store_transposed(o_ref, pl.multiple_of(r0 + n * ATT_SUB, ATT_SUB), acc)
        return 0

    lax.fori_loop(0, seq // ATT_TQ, q_block, 0)


def _head_spec(rows, col_block0):
    return pl.BlockSpec((rows, HEAD_DIM), lambda b, h: (b, col_block0 + h))


def _meta_spec(n_real, col_block0):
    return pl.BlockSpec((N_META, HEAD_DIM), lambda b, h: (n_real // N_META, col_block0 + h))


def _fox_attention(qkv, qc, kc, *, n_batch, seq):
    n_real = n_batch * seq
    shared = pl.BlockSpec((seq, LANES), lambda b, h: (b, 0))
    return pl.pallas_call(
        functools.partial(_fox_kernel, seq=seq),
        grid=(n_batch, N_HEADS),
        in_specs=[_head_spec(seq, 0), shared, _head_spec(seq, N_HEADS), shared, _head_spec(seq, 2 * N_HEADS),
                  _meta_spec(n_real, N_HEADS), pl.BlockSpec((N_META, LANES), lambda b, h: (n_real // N_META, 0)),
                  _meta_spec(n_real, 2 * N_HEADS)],
        out_specs=_head_spec(seq, 0),
        out_shape=jax.ShapeDtypeStruct((n_real, WIDTH), BF16),
        scratch_shapes=[pltpu.VMEM((LANES, 2 * HEAD_DIM), BF16), pltpu.VMEM((HEAD_DIM, seq + LANES), BF16),
                        pltpu.VMEM((ATT_NSUB, ATT_SUB, ATT_SUB), F32), pltpu.VMEM((ATT_NSUB, ATT_SUB, ATT_SUB), F32)],
        compiler_params=_params(2),
        name="fox_attention",
    )(qkv, qc, qkv, kc, qkv, qkv, kc, qkv)


def _sb_attention(qkv, *, n_batch, seq):
    n_real = n_batch * seq
    return pl.pallas_call(
        functools.partial(_sb_kernel, seq=seq),
        grid=(n_batch, N_HEADS),
        in_specs=[_head_spec(seq, 0), _head_spec(seq, N_HEADS), _head_spec(seq, 2 * N_HEADS),
                  _meta_spec(n_real, N_HEADS), _meta_spec(n_real, 2 * N_HEADS)],
        out_specs=_head_spec(seq, 0),
        out_shape=jax.ShapeDtypeStruct((n_real, WIDTH), BF16),
        scratch_shapes=[pltpu.VMEM((LANES, HEAD_DIM), BF16), pltpu.VMEM((HEAD_DIM, seq + LANES), BF16),
                        pltpu.VMEM((2 * ATT_NSUB, ATT_SUB, ATT_SUB), F32)],
        compiler_params=_params(2),
        name="sb_attention",
    )(qkv, qkv, qkv, qkv, qkv)


def _mix_merge_kernel(of_ref, os_ref, gf_ref, gs_ref, wbf_ref, wbs_ref, o_ref, wbf_b, wbs_b):
    @pl.when(pl.program_id(1) == 0)
    def _():
        wbf_b[...] = wbf_ref[...].astype(BF16)
        wbs_b[...] = wbs_ref[...].astype(BF16)

    for rows in _row_chunks(of_ref.shape[0]):
        tf = _mm(of_ref[rows, :], wbf_b[...])
        ts = _mm(os_ref[rows, :], wbs_b[...])
        merged = gf_ref[rows, :].astype(F32) * tf + gs_ref[rows, :].astype(F32) * ts
        o_ref[rows, :] = merged.astype(o_ref.dtype)


def _mix_merge(o_fox, o_sb, gates, wbf, wbs, *, rows, tm, tn):
    ni, nj = rows // tm, D_MODEL // tn
    return pl.pallas_call(
        _mix_merge_kernel,
        grid=(nj, ni),
        in_specs=[
            pl.BlockSpec((tm, WIDTH), lambda j, i: (i, 0)),
            pl.BlockSpec((tm, WIDTH), lambda j, i: (i, 0)),
            pl.BlockSpec((tm, tn), lambda j, i: (i, j)),
            pl.BlockSpec((tm, tn), lambda j, i: (i, nj + j)),
            pl.BlockSpec((WIDTH, tn), lambda j, i: (0, j)),
            pl.BlockSpec((WIDTH, tn), lambda j, i: (0, j)),
        ],
        out_specs=pl.BlockSpec((tm, tn), lambda j, i: (i, j)),
        out_shape=jax.ShapeDtypeStruct((rows, D_MODEL), BF16),
        scratch_shapes=[pltpu.VMEM((WIDTH, tn), BF16), pltpu.VMEM((WIDTH, tn), BF16)],
        compiler_params=_params(2),
        name="mix_merge",
    )(o_fox, o_sb, gates, gates, wbf, wbs)


def _mix_proj_kernel(m_ref, w_ref, h_ref, o_ref, wb_ref):
    @pl.when(pl.program_id(1) == 0)
    def _():
        wb_ref[...] = w_ref[...].astype(BF16)

    for rows in _row_chunks(m_ref.shape[0]):
        o_ref[rows, :] = h_ref[rows, :] + _mm(m_ref[rows, :], wb_ref[...])


def _mix_proj(merged, wo, h, *, rows, tm, tn):
    ni, nj = rows // tm, D_MODEL // tn
    return pl.pallas_call(
        _mix_proj_kernel,
        grid=(nj, ni),
        in_specs=[
            pl.BlockSpec((tm, D_MODEL), lambda j, i: (i, 0)),
            pl.BlockSpec((D_MODEL, tn), lambda j, i: (0, j)),
            pl.BlockSpec((tm, tn), lambda j, i: (i, j)),
        ],
        out_specs=pl.BlockSpec((tm, tn), lambda j, i: (i, j)),
        out_shape=jax.ShapeDtypeStruct((rows, D_MODEL), F32),
        scratch_shapes=[pltpu.VMEM((D_MODEL, tn), BF16)],
        compiler_params=_params(2),
        name="mix_proj",
    )(merged, wo, h)


def kernel(x, meta_tokens, ffn1_norm, ffn1_w_gate, ffn1_w_up, ffn1_w_down, mix_norm, w_in, b_forget, fox_q_norm, fox_k_norm, w_branch_fox, w_branch_sb, w_out, ffn2_norm, ffn2_w_gate, ffn2_w_up, ffn2_w_down):
    n_batch, seq, _ = x.shape
    n_real = n_batch * seq
    n_all = n_real + N_META
    assert ffn1_norm.shape[0] == 1
    assert n_all % ROW_TILE_ALL == 0 and n_real % ROW_TILE_REAL == 0 and seq % ATT_TQ == 0

    wt = w_in[0].T
    r_f = 3 * WIDTH
    r_sb = r_f + N_HEADS
    r_gate = r_sb + 3 * WIDTH
    wt_f = jnp.pad(wt[r_f:r_sb], ((0, LANES - N_HEADS), (0, 0)))
    b_f = jnp.pad(b_forget[0].reshape(1, N_HEADS), ((0, 0), (0, LANES - N_HEADS)))
    fox_gain = jnp.concatenate([fox_q_norm[0].reshape(1, WIDTH) * (SCALE * LOG2E), fox_k_norm[0].reshape(1, WIDTH),
                                jnp.ones((1, WIDTH), F32)], axis=1)
    sb_scale = jnp.concatenate([jnp.full((1, WIDTH), SCALE * LOG2E, F32), jnp.ones((1, 2 * WIDTH), F32)], axis=1)

    h1, n1 = _ffn(x.reshape(n_real, D_MODEL), meta_tokens.astype(F32), ffn1_norm[0], ffn1_w_gate[0],
                  ffn1_w_up[0], ffn1_w_down[0], rows=n_all, tm=ROW_TILE_ALL, next_gain=mix_norm[0])

    proj = functools.partial(_proj, n1, rows=n_all, tm=ROW_TILE_ALL)
    fox_qkv = proj(wt, 0, 3 * WIDTH, fox_gain, tn=PROJ_TILE_N, epilogue=_fox_epilogue,
                   out_dtype=BF16, name="proj_fox")
    log_f = proj(wt_f, 0, LANES, b_f, tn=LANES, epilogue=_logf_epilogue, out_dtype=F32, name="proj_logf")
    sb_qkv = proj(wt, r_sb, 3 * WIDTH, sb_scale, tn=PROJ_TILE_N, epilogue=_scale_epilogue,
                  out_dtype=BF16, name="proj_sb")
    gates = _proj(n1, wt, r_gate, 2 * D_MODEL, None, rows=n_real, tm=ROW_TILE_REAL,
                  tn=PROJ_TILE_N, epilogue=_gate_epilogue, out_dtype=BF16, name="proj_gates")

    qc, kc = _decay(log_f, n_batch=n_batch, seq=seq)
    o_fox = _fox_attention(fox_qkv, qc, kc, n_batch=n_batch, seq=seq)
    o_sb = _sb_attention(sb_qkv, n_batch=n_batch, seq=seq)

    merged = _mix_merge(o_fox, o_sb, gates, w_branch_fox[0], w_branch_sb[0],
                        rows=n_real, tm=ROW_TILE_REAL, tn=PROJ_TILE_N)
    h2 = _mix_proj(merged, w_out[0], h1, rows=n_real, tm=ROW_TILE_REAL, tn=PROJ_TILE_N)
    h3 = _ffn(h2, None, ffn2_norm[0], ffn2_w_gate[0], ffn2_w_up[0], ffn2_w_down[0],
              rows=n_real, tm=ROW_TILE_REAL)
    return h3.reshape(n_batch, seq, D_MODEL)
```

```python
import functools
import math

import jax
import jax.numpy as jnp
from jax import lax
from jax.experimental import pallas as pl
from jax.experimental.pallas import tpu as pltpu

F32 = jnp.float32
BF16 = jnp.bfloat16

D_MODEL = 2048
D_FF = 5632
N_META = 16
HEAD_DIM = 128
N_HEADS = 8
WIDTH = N_HEADS * HEAD_DIM
RMS_EPS = 1e-6
FFN_RESIDUAL_WEIGHT = 0.5
SCALE = HEAD_DIM ** -0.5
LOG2E = math.log2(math.e)

LANES = 128
SUBLANES = 8
BF16_ROWS = 16
VMEM_LIMIT = 58 * 2**20

ROW_TILE_ALL = 912
ROW_TILE_REAL = 1024
FFN_TILE_F = 256
PROJ_TILE_N = 1024
OUT_TILE_N = 256
ATT_SUB = 256
ATT_NSUB = 4
ATT_TQ = ATT_SUB * ATT_NSUB
DECAY_TERMS = 3
SUFFIX_TERMS = 1
MAX_EXP2 = 126.0


def _mm(a, b):
    return jnp.dot(a, b, preferred_element_type=F32)


def _mm_nt(a, b):
    return lax.dot_general(a, b, (((1,), (1,)), ((), ())), preferred_element_type=F32)


def _split_bf16(x, n):
    parts = [x.astype(BF16)]
    for _ in range(n - 1):
        x = x - parts[-1].astype(F32)
        parts.append(x.astype(BF16))
    return parts


def _rmsnorm_rows(h, gain):
    ms = jnp.mean(h * h, axis=-1, keepdims=True)
    return h * lax.rsqrt(ms + RMS_EPS) * gain


def _log_sigmoid(z):
    return jnp.minimum(z, 0.0) - jnp.log(1.0 + jnp.exp(-jnp.abs(z)))


def _params(n_grid_axes):
    return pltpu.CompilerParams(dimension_semantics=("arbitrary",) * n_grid_axes,
                                vmem_limit_bytes=VMEM_LIMIT)


def _ffn_kernel(*refs, meta_row, emit_norm):
    refs = list(refs)
    x_ref = refs.pop(0)
    meta_ref = refs.pop(0) if meta_row is not None else None
    gain_ref, wg_ref, wu_ref, wd_ref = refs[:4]
    refs = refs[4:]
    if emit_norm:
        ngain_ref, o_ref, n_ref, xn_ref = refs
    else:
        o_ref, xn_ref = refs
    i = pl.program_id(0)
    j = pl.program_id(1)

    @pl.when(j == 0)
    def _():
        if meta_row is None:
            o_ref[...] = x_ref[...]
        else:
            last = pl.num_programs(0) - 1

            @pl.when(i < last)
            def _():
                o_ref[...] = x_ref[...]

            @pl.when(i == last)
            def _():
                o_ref[0:meta_row, :] = x_ref[0:meta_row, :]
                o_ref[meta_row:meta_row + N_META, :] = meta_ref[...]
        xn_ref[...] = _rmsnorm_rows(o_ref[...], gain_ref[...]).astype(BF16)

    xn = xn_ref[...]
    g = _mm(xn, wg_ref[...].astype(BF16))
    u = _mm(xn, wu_ref[...].astype(BF16))
    a = (g * jax.nn.sigmoid(g)) * (u * FFN_RESIDUAL_WEIGHT)
    o_ref[...] += _mm(a.astype(BF16), wd_ref[...].astype(BF16))

    if emit_norm:
        @pl.when(j == pl.num_programs(1) - 1)
        def _():
            n_ref[...] = _rmsnorm_rows(o_ref[...], ngain_ref[...]).astype(BF16)


def _ffn(x, meta, gain, wg, wu, wd, *, rows, tm, next_gain=None):
    ni = rows // tm
    nj = D_FF // FFN_TILE_F
    emit_norm = next_gain is not None
    in_specs = [pl.BlockSpec((tm, D_MODEL), lambda i, j: (i, 0))]
    args = [x]
    meta_row = None
    if meta is not None:
        meta_row = x.shape[0] - (ni - 1) * tm
        assert meta_row + N_META == tm
        in_specs.append(pl.BlockSpec((N_META, D_MODEL), lambda i, j: (0, 0)))
        args.append(meta)
    in_specs += [
        pl.BlockSpec((1, D_MODEL), lambda i, j: (0, 0)),
        pl.BlockSpec((D_MODEL, FFN_TILE_F), lambda i, j: (0, j)),
        pl.BlockSpec((D_MODEL, FFN_TILE_F), lambda i, j: (0, j)),
        pl.BlockSpec((FFN_TILE_F, D_MODEL), lambda i, j: (j, 0)),
    ]
    args += [gain.reshape(1, D_MODEL), wg, wu, wd]
    out_shape = [jax.ShapeDtypeStruct((rows, D_MODEL), F32)]
    out_specs = [pl.BlockSpec((tm, D_MODEL), lambda i, j: (i, 0))]
    if emit_norm:
        in_specs.append(pl.BlockSpec((1, D_MODEL), lambda i, j: (0, 0)))
        args.append(next_gain.reshape(1, D_MODEL))
        out_shape.append(jax.ShapeDtypeStruct((rows, D_MODEL), BF16))
        out_specs.append(pl.BlockSpec((tm, D_MODEL), lambda i, j: (i, 0)))
    res = pl.pallas_call(
        functools.partial(_ffn_kernel, meta_row=meta_row, emit_norm=emit_norm),
        grid=(ni, nj),
        in_specs=in_specs,
        out_specs=out_specs,
        out_shape=out_shape,
        scratch_shapes=[pltpu.VMEM((tm, D_MODEL), BF16)],
        compiler_params=_params(2),
        name="ffn_norm" if emit_norm else "ffn",
    )(*args)
    return res if emit_norm else res[0]


def _proj_kernel(n_ref, w_ref, *refs, epilogue):
    p_ref = refs[0] if len(refs) == 3 else None
    o_ref, wb_ref = refs[-2:]

    @pl.when(pl.program_id(1) == 0)
    def _():
        wb_ref[...] = w_ref[...].T.astype(BF16)

    def run(fn):
        for rows in _row_chunks(n_ref.shape[0]):
            o_ref[rows, :] = fn(_mm(n_ref[rows, :], wb_ref[...])).astype(o_ref.dtype)

    epilogue(run, p_ref, pl.program_id(0))


def _row_chunks(tm):
    n = next(n for n in (4, 3, 2, 1) if tm % (n * BF16_ROWS) == 0)
    return [slice(c * (tm // n), (c + 1) * (tm // n)) for c in range(n)]


def _fox_epilogue(run, p_ref, j):
    def head_norm(y):
        heads = [slice(h * HEAD_DIM, (h + 1) * HEAD_DIM) for h in range(y.shape[1] // HEAD_DIM)]
        return jnp.concatenate([_rmsnorm_rows(y[:, sl], p_ref[:, sl]) for sl in heads], axis=1)

    pl.when(j < 2)(lambda: run(head_norm))
    pl.when(j >= 2)(lambda: run(lambda y: y))


def _scale_epilogue(run, p_ref, j):
    run(lambda y: y * p_ref[...])


def _gate_epilogue(run, p_ref, j):
    run(jax.nn.sigmoid)


def _logf_epilogue(run, p_ref, j):
    run(lambda y: _log_sigmoid(y + p_ref[...]) * LOG2E)


def _proj(n, wt, row0, n_cols, p, *, rows, tm, tn, epilogue, out_dtype, name):
    assert n_cols % tn == 0
    nj, ni = n_cols // tn, rows // tm
    if row0 % tn == 0:
        w_spec = pl.BlockSpec((tn, D_MODEL), lambda j, i: (row0 // tn + j, 0))
    else:
        assert row0 % SUBLANES == 0 and tn % SUBLANES == 0
        w_spec = pl.BlockSpec((pl.Element(tn), pl.Element(D_MODEL)),
                              lambda j, i: ((row0 // SUBLANES + j * (tn // SUBLANES)) * SUBLANES, 0))
    in_specs = [pl.BlockSpec((tm, D_MODEL), lambda j, i: (i, 0)), w_spec]
    args = [n, wt]
    if p is not None:
        in_specs.append(pl.BlockSpec((1, tn), lambda j, i: (0, j)))
        args.append(p)
    return pl.pallas_call(
        functools.partial(_proj_kernel, epilogue=epilogue),
        grid=(nj, ni),
        in_specs=in_specs,
        out_specs=pl.BlockSpec((tm, tn), lambda j, i: (i, j)),
        out_shape=jax.ShapeDtypeStruct((rows, n_cols), out_dtype),
        scratch_shapes=[pltpu.VMEM((D_MODEL, tn), BF16)],
        compiler_params=_params(2),
        name=name,
    )(*args)


def _decay_kernel(lf_ref, qc_ref, kc_ref, c_ref, *, n_batch, seq):
    T = DECAY_TERMS
    n_real = n_batch * seq
    r_i = lax.broadcasted_iota(jnp.int32, (LANES, LANES), 0)
    c_i = lax.broadcasted_iota(jnp.int32, (LANES, LANES), 1)
    tri = jnp.where(r_i >= c_i, 1.0, 0.0).astype(BF16)

    def cum_block(x, carry):
        acc = carry
        for p in _split_bf16(x, T):
            acc = acc + _mm(tri, p)
        return acc

    xm = jnp.concatenate([lf_ref[n_real:n_real + N_META, :], jnp.zeros((LANES - N_META, LANES), F32)], axis=0)
    cm = cum_block(xm, jnp.zeros((1, LANES), F32))
    c_ref[n_real:n_real + N_META, :] = cm[:N_META, :]
    base = cm[N_META - 1:N_META, :]

    for b in range(n_batch):
        def body(t, carry, b=b):
            r0 = pl.multiple_of(b * seq + t * LANES, LANES)
            c = cum_block(lf_ref[pl.ds(r0, LANES), :], carry)
            c_ref[pl.ds(r0, LANES), :] = c
            return c[LANES - 1:LANES, :]
        lax.fori_loop(0, seq // LANES, body, base)

    k_i = lax.broadcasted_iota(jnp.int32, (T * LANES, LANES), 0)
    n_i = lax.broadcasted_iota(jnp.int32, (T * LANES, LANES), 1)
    sel_q = jnp.zeros((T * LANES, LANES), F32)
    sel_k = jnp.zeros((T * LANES, LANES), F32)
    lane = lax.broadcasted_iota(jnp.int32, (1, LANES), 1)
    ones_q = jnp.zeros((1, LANES), F32)
    ones_k = jnp.zeros((1, LANES), F32)
    for h in range(N_HEADS):
        for p in range(T):
            row = p * LANES + h
            sel_q = jnp.where((k_i == row) & (n_i == 2 * T * h + p), 1.0, sel_q)
            sel_k = jnp.where((k_i == row) & (n_i == 2 * T * h + T + p), -1.0, sel_k)
            ones_q = jnp.where(lane == 2 * T * h + T + p, 1.0, ones_q)
            ones_k = jnp.where(lane == 2 * T * h + p, 1.0, ones_k)
    sel_q = sel_q.astype(BF16)
    sel_k = sel_k.astype(BF16)

    def expand(c):
        pieces = jnp.concatenate(_split_bf16(c, T), axis=1)
        return ((_mm(pieces, sel_q) + ones_q).astype(BF16), (_mm(pieces, sel_k) + ones_k).astype(BF16))

    chunk = 1024
    def ebody(t, _):
        r0 = pl.multiple_of(t * chunk, chunk)
        q, k = expand(c_ref[pl.ds(r0, chunk), :])
        qc_ref[pl.ds(r0, chunk), :] = q
        kc_ref[pl.ds(r0, chunk), :] = k
        return 0
    lax.fori_loop(0, n_real // chunk, ebody, 0)
    _, km = expand(c_ref[n_real:n_real + N_META, :])
    kc_ref[n_real:n_real + N_META, :] = km


def _decay(lf, *, n_batch, seq):
    n_real = n_batch * seq
    return pl.pallas_call(
        functools.partial(_decay_kernel, n_batch=n_batch, seq=seq),
        out_shape=[jax.ShapeDtypeStruct((n_real, LANES), BF16),
                   jax.ShapeDtypeStruct((n_real + N_META, LANES), BF16)],
        scratch_shapes=[pltpu.VMEM((n_real + N_META, LANES), F32)],
        compiler_params=pltpu.CompilerParams(vmem_limit_bytes=VMEM_LIMIT),
        name="decay",
    )(lf)


def _eye(n):
    return jnp.where(lax.broadcasted_iota(jnp.int32, (n, n), 0) == lax.broadcasted_iota(jnp.int32, (n, n), 1),
                     1.0, 0.0).astype(BF16)


def _stage_keys_values(kpad_ref, vt_ref, meta_k_refs, v_ref, mv_ref, *, seq):
    eye = _eye(HEAD_DIM)
    kpad_ref[...] = jnp.zeros_like(kpad_ref)
    for n, ref in enumerate(meta_k_refs):
        kpad_ref[0:N_META, n * HEAD_DIM:(n + 1) * HEAD_DIM] = ref[...]
    vpad = jnp.concatenate([mv_ref[...], jnp.zeros((LANES - N_META, HEAD_DIM), BF16)], axis=0)
    vt_ref[:, seq:seq + LANES] = _mm_nt(eye, vpad).astype(BF16)
    chunk = 512

    def body(c, _):
        c0 = pl.multiple_of(c * chunk, chunk)
        vt_ref[:, pl.ds(c0, chunk)] = _mm_nt(eye, v_ref[pl.ds(c0, chunk), :]).astype(BF16)
        return 0
    lax.fori_loop(0, seq // chunk, body, 0)


def _store_transposed(o_ref, r0, acc_t):
    o_ref[pl.ds(r0, ATT_SUB), :] = acc_t.T.astype(o_ref.dtype)


def _fox_head(q_ref, qc_ref, k_ref, kc_ref, v_ref, mk_ref, mkc_ref, mv_ref, o_ref, kpad_ref, vt_ref,
              sa_ref, sb_ref, *, seq):
    T = DECAY_TERMS
    h = pl.program_id(1)
    _stage_keys_values(kpad_ref, vt_ref, (mk_ref, mkc_ref), v_ref, mv_ref, seq=seq)

    lane = lax.broadcasted_iota(jnp.int32, (1, LANES), 1)
    head_lanes = (lane >= 2 * T * h) & (lane < 2 * T * (h + 1))
    meta_valid = lax.broadcasted_iota(jnp.int32, (LANES, ATT_SUB), 0) < N_META
    causal = (lax.broadcasted_iota(jnp.int32, (ATT_SUB, ATT_SUB), 0)
              <= lax.broadcasted_iota(jnp.int32, (ATT_SUB, ATT_SUB), 1))
    all_subs = tuple(range(ATT_NSUB))

    def keys(c0):
        return jnp.concatenate([k_ref[pl.ds(c0, ATT_SUB), :], kc_ref[pl.ds(c0, ATT_SUB), :]], axis=1)

    def online(s, vt, m, l, acc):
        m_new = jnp.maximum(m, jnp.max(s, axis=0, keepdims=True))
        alpha = jnp.exp2(m - m_new)
        p = jnp.exp2(s - m_new)
        l = alpha * l + jnp.sum(p, axis=0, keepdims=True)
        acc = alpha * acc + _mm(vt, p.astype(BF16))
        return m_new, l, acc

    def block(i):
        r0 = pl.multiple_of(i * ATT_TQ, ATT_TQ)
        qx = jnp.where(head_lanes, qc_ref[pl.ds(r0, ATT_TQ), :], jnp.zeros((), BF16))
        qa = jnp.concatenate([q_ref[pl.ds(r0, ATT_TQ), :], qx], axis=1)
        subs = [qa[n * ATT_SUB:(n + 1) * ATT_SUB] for n in all_subs]

        def issue(buf_ref, c0, which=all_subs):
            ka = keys(c0)
            for n in which:
                buf_ref[n] = _mm_nt(ka, subs[n])

        def consume(buf_ref, c0, state, masked=None, which=all_subs):
            vt = vt_ref[:, pl.ds(c0, ATT_SUB)]
            out = list(state)
            for n in which:
                s = buf_ref[n]
                if n == masked:
                    s = jnp.where(causal, s, -jnp.inf)
                out[n] = online(s, vt, *out[n])
            return tuple(out)

        issue(sa_ref, 0)

        state = []
        for qs in subs:
            s = jnp.where(meta_valid, _mm_nt(kpad_ref[...], qs), -jnp.inf)
            m = jnp.max(s, axis=0, keepdims=True)
            p = jnp.exp2(s - m)
            state.append((m, jnp.sum(p, axis=0, keepdims=True), _mm(vt_ref[:, seq:seq + LANES], p.astype(BF16))))

        def kv_pair(t, state):
            c0 = pl.multiple_of(t * (2 * ATT_SUB), 2 * ATT_SUB)
            c1 = pl.multiple_of(c0 + ATT_SUB, ATT_SUB)
            issue(sb_ref, c1)
            state = consume(sa_ref, c0, state)
            issue(sa_ref, pl.multiple_of(c0 + 2 * ATT_SUB, 2 * ATT_SUB))
            return consume(sb_ref, c1, state)

        def finish(state):
            bufs = (sa_ref, sb_ref)
            for kb in all_subs:
                if kb + 1 < ATT_NSUB:
                    issue(bufs[(kb + 1) % 2], pl.multiple_of(r0 + (kb + 1) * ATT_SUB, ATT_SUB), all_subs[kb + 1:])
                state = consume(bufs[kb % 2], pl.multiple_of(r0 + kb * ATT_SUB, ATT_SUB), state, kb, all_subs[kb:])
            for n, (m, l, acc) in enumerate(state):
                _store_transposed(o_ref, pl.multiple_of(r0 + n * ATT_SUB, ATT_SUB), acc / l)

        return tuple(state), kv_pair, finish

    return block


def _sb_head(q_ref, k_ref, v_ref, mk_ref, mv_ref, o_ref, kpad_ref, vt_ref, z_ref, *, seq):
    _stage_keys_values(kpad_ref, vt_ref, (mk_ref,), v_ref, mv_ref, seq=seq)

    def from_here_on(n):
        return jnp.where(lax.broadcasted_iota(jnp.int32, (n, n), 0) <= lax.broadcasted_iota(jnp.int32, (n, n), 1),
                         1.0, 0.0).astype(BF16)

    u_blk = from_here_on(ATT_SUB)
    u_meta = from_here_on(LANES)
    meta_valid = lax.broadcasted_iota(jnp.int32, (LANES, ATT_SUB), 0) < N_META
    strict = (lax.broadcasted_iota(jnp.int32, (ATT_SUB, ATT_SUB), 0)
              < lax.broadcasted_iota(jnp.int32, (ATT_SUB, ATT_SUB), 1))
    all_subs = tuple(range(ATT_NSUB))

    def steps(items, states, zs=None):
        states = list(states)
        if zs is None:
            zs = [_mm_nt(kblk, qh) for _, qh, kblk, _, _, _ in items]
        cums = []
        for z, (_, _, _, _, u, mask) in zip(zs, items):
            sp = jnp.maximum(jnp.log2(1.0 + jnp.exp2(jnp.minimum(z, MAX_EXP2))), z)
            if mask is not None:
                sp = jnp.where(mask, sp, 0.0)
            pieces = _split_bf16(sp, SUFFIX_TERMS)
            cums.append(functools.reduce(lambda a, b: a + b, [_mm(u, p) for p in pieces]))
        for z, cum, (half, _, _, vt, _, mask) in zip(zs, cums, items):
            r, acc = states[half]
            w = jnp.exp2(z - cum - r)
            if mask is not None:
                w = jnp.where(mask, w, 0.0)
            states[half] = (r + cum[0:1, :], acc + _mm(vt, w.astype(BF16)))
        return tuple(states)

    def block(i):
        r0 = pl.multiple_of(i * ATT_TQ, ATT_TQ)
        q = [q_ref[pl.ds(pl.multiple_of(r0 + n * ATT_SUB, ATT_SUB), ATT_SUB), :] for n in all_subs]
        zero = (jnp.zeros((1, ATT_SUB), F32), jnp.zeros((HEAD_DIM, ATT_SUB), F32))

        def item(n, c0, mask):
            return (n, q[n], k_ref[pl.ds(c0, ATT_SUB), :], vt_ref[:, pl.ds(c0, ATT_SUB)], u_blk, mask)

        def pair_items(jj):
            items = []
            for d in (1, 2):
                c0 = pl.multiple_of(jnp.maximum(r0 - (2 * jj + d) * ATT_SUB, 0), ATT_SUB)
                items += [item(n, c0, None) for n in all_subs]
            return items

        def scores(items):
            return [_mm_nt(kblk, qs) for _, qs, kblk, _, _, _ in items]

        for n, z in enumerate(scores(pair_items(0))):
            z_ref[n] = z

        diag = []
        for kb in reversed(all_subs):
            c0 = pl.multiple_of(r0 + kb * ATT_SUB, ATT_SUB)
            diag += [item(n, c0, strict if n == kb else None) for n in all_subs[kb:]]
        states = steps(diag, (zero,) * ATT_NSUB)

        def kv_pair(jj, states):
            zs = [z_ref[n] for n in range(2 * ATT_NSUB)]
            ahead = scores(pair_items(jj + 1))
            states = steps(pair_items(jj), states, zs)
            for n, z in enumerate(ahead):
                z_ref[n] = z
            return states

        def finish(states):
            meta = [(n, q[n], kpad_ref[...], vt_ref[:, seq:seq + LANES], u_meta, meta_valid) for n in all_subs]
            for n, (_, acc) in enumerate(steps(meta, states)):
                _store_transposed(o_ref, pl.multiple_of(r0 + n * ATT_SUB, ATT_SUB), acc)

        return states, kv_pair, finish

    return block


def _attention_kernel(fq, fqc, fk, fkc, fv, fmk, fmkc, fmv, sq, sk, sv, smk, smv, of_ref, os_ref,
                      f_kpad, f_vt, f_sa, f_sb, s_kpad, s_vt, s_z, *, seq):
    fox = _fox_head(fq, fqc, fk, fkc, fv, fmk, fmkc, fmv, of_ref, f_kpad, f_vt, f_sa, f_sb, seq=seq)
    stick = _sb_head(sq, sk, sv, smk, smv, os_ref, s_kpad, s_vt, s_z, seq=seq)

    def q_block(i, _):
        f_state, f_pair, f_finish = fox(i)
        s_state, s_pair, s_finish = stick(i)

        def kv_pair(t, carry):
            s_next = s_pair(t, carry[1])
            return f_pair(t, carry[0]), s_next

        f_state, s_state = lax.fori_loop(0, i * (ATT_NSUB // 2), kv_pair, (f_state, s_state))
        f_finish(f_state)
        s_finish(s_state)
        return 0

    lax.fori_loop(0, seq // ATT_TQ, q_block, 0)


def _head_spec(rows, col_block0):
    return pl.BlockSpec((rows, HEAD_DIM), lambda b, h: (b, col_block0 + h))


def _meta_spec(n_real, col_block0):
    return pl.BlockSpec((N_META, HEAD_DIM), lambda b, h: (n_real // N_META, col_block0 + h))


def _attention(fox_qkv, qc, kc, sb_qkv, *, n_batch, seq):
    n_real = n_batch * seq
    shared = pl.BlockSpec((seq, LANES), lambda b, h: (b, 0))
    qkv_specs = [_head_spec(seq, 0), _head_spec(seq, N_HEADS), _head_spec(seq, 2 * N_HEADS)]
    meta_specs = [_meta_spec(n_real, N_HEADS), _meta_spec(n_real, 2 * N_HEADS)]
    score_buf = pltpu.VMEM((ATT_NSUB, ATT_SUB, ATT_SUB), F32)
    return pl.pallas_call(
        functools.partial(_attention_kernel, seq=seq),
        grid=(n_batch, N_HEADS),
        in_specs=[qkv_specs[0], shared, qkv_specs[1], shared, qkv_specs[2],
                  meta_specs[0], pl.BlockSpec((N_META, LANES), lambda b, h: (n_real // N_META, 0)), meta_specs[1],
                  *qkv_specs, *meta_specs],
        out_specs=[_head_spec(seq, 0), _head_spec(seq, 0)],
        out_shape=[jax.ShapeDtypeStruct((n_real, WIDTH), BF16)] * 2,
        scratch_shapes=[pltpu.VMEM((LANES, 2 * HEAD_DIM), BF16), pltpu.VMEM((HEAD_DIM, seq + LANES), BF16),
                        score_buf, score_buf,
                        pltpu.VMEM((LANES, HEAD_DIM), BF16), pltpu.VMEM((HEAD_DIM, seq + LANES), BF16),
                        pltpu.VMEM((2 * ATT_NSUB, ATT_SUB, ATT_SUB), F32)],
        compiler_params=_params(2),
        name="attention",
    )(fox_qkv, qc, fox_qkv, kc, fox_qkv, fox_qkv, kc, fox_qkv, sb_qkv, sb_qkv, sb_qkv, sb_qkv, sb_qkv)


def _mix_merge_kernel(of_ref, os_ref, gf_ref, gs_ref, wbf_ref, wbs_ref, o_ref, wbf_b, wbs_b):
    @pl.when(pl.program_id(1) == 0)
    def _():
        wbf_b[...] = wbf_ref[...].astype(BF16)
        wbs_b[...] = wbs_ref[...].astype(BF16)

    for rows in _row_chunks(of_ref.shape[0]):
        tf = _mm(of_ref[rows, :], wbf_b[...])
        ts = _mm(os_ref[rows, :], wbs_b[...])
        merged = gf_ref[rows, :].astype(F32) * tf + gs_ref[rows, :].astype(F32) * ts
        o_ref[rows, :] = merged.astype(o_ref.dtype)


def _mix_merge(o_fox, o_sb, gates, wbf, wbs, *, rows, tm, tn):
    ni, nj = rows // tm, D_MODEL // tn
    return pl.pallas_call(
        _mix_merge_kernel,
        grid=(nj, ni),
        in_specs=[
            pl.BlockSpec((tm, WIDTH), lambda j, i: (i, 0)),
            pl.BlockSpec((tm, WIDTH), lambda j, i: (i, 0)),
            pl.BlockSpec((tm, tn), lambda j, i: (i, j)),
            pl.BlockSpec((tm, tn), lambda j, i: (i, nj + j)),
            pl.BlockSpec((WIDTH, tn), lambda j, i: (0, j)),
            pl.BlockSpec((WIDTH, tn), lambda j, i: (0, j)),
        ],
        out_specs=pl.BlockSpec((tm, tn), lambda j, i: (i, j)),
        out_shape=jax.ShapeDtypeStruct((rows, D_MODEL), BF16),
        scratch_shapes=[pltpu.VMEM((WIDTH, tn), BF16), pltpu.VMEM((WIDTH, tn), BF16)],
        compiler_params=_params(2),
        name="mix_merge",
    )(o_fox, o_sb, gates, gates, wbf, wbs)


def _mix_proj_kernel(m_ref, w_ref, h_ref, o_ref, wb_ref):
    @pl.when(pl.program_id(1) == 0)
    def _():
        wb_ref[...] = w_ref[...].astype(BF16)

    for rows in _row_chunks(m_ref.shape[0]):
        o_ref[rows, :] = h_ref[rows, :] + _mm(m_ref[rows, :], wb_ref[...])


def _mix_proj(merged, wo, h, *, rows, tm, tn):
    ni, nj = rows // tm, D_MODEL // tn
    return pl.pallas_call(
        _mix_proj_kernel,
        grid=(nj, ni),
        in_specs=[
            pl.BlockSpec((tm, D_MODEL), lambda j, i: (i, 0)),
            pl.BlockSpec((D_MODEL, tn), lambda j, i: (0, j)),
            pl.BlockSpec((tm, tn), lambda j, i: (i, j)),
        ],
        out_specs=pl.BlockSpec((tm, tn), lambda j, i: (i, j)),
        out_shape=jax.ShapeDtypeStruct((rows, D_MODEL), F32),
        scratch_shapes=[pltpu.VMEM((D_MODEL, tn), BF16)],
        compiler_params=_params(2),
        name="mix_proj",
    )(merged, wo, h)


def kernel(x, meta_tokens, ffn1_norm, ffn1_w_gate, ffn1_w_up, ffn1_w_down, mix_norm, w_in, b_forget, fox_q_norm, fox_k_norm, w_branch_fox, w_branch_sb, w_out, ffn2_norm, ffn2_w_gate, ffn2_w_up, ffn2_w_down):
    n_batch, seq, _ = x.shape
    n_real = n_batch * seq
    n_all = n_real + N_META
    assert ffn1_norm.shape[0] == 1
    assert n_all % ROW_TILE_ALL == 0 and n_real % ROW_TILE_REAL == 0 and seq % ATT_TQ == 0

    wt = w_in[0].T
    r_f = 3 * WIDTH
    r_sb = r_f + N_HEADS
    r_gate = r_sb + 3 * WIDTH
    wt_f = jnp.pad(wt[r_f:r_sb], ((0, LANES - N_HEADS), (0, 0)))
    b_f = jnp.pad(b_forget[0].reshape(1, N_HEADS), ((0, 0), (0, LANES - N_HEADS)))
    fox_gain = jnp.concatenate([fox_q_norm[0].reshape(1, WIDTH) * (SCALE * LOG2E), fox_k_norm[0].reshape(1, WIDTH),
                                jnp.ones((1, WIDTH), F32)], axis=1)
    sb_scale = jnp.concatenate([jnp.full((1, WIDTH), SCALE * LOG2E, F32), jnp.ones((1, 2 * WIDTH), F32)], axis=1)

    h1, n1 = _ffn(x.reshape(n_real, D_MODEL), meta_tokens.astype(F32), ffn1_norm[0], ffn1_w_gate[0],
                  ffn1_w_up[0], ffn1_w_down[0], rows=n_all, tm=ROW_TILE_ALL, next_gain=mix_norm[0])

    proj = functools.partial(_proj, n1, rows=n_all, tm=ROW_TILE_ALL)
    fox_qkv = proj(wt, 0, 3 * WIDTH, fox_gain, tn=PROJ_TILE_N, epilogue=_fox_epilogue,
                   out_dtype=BF16, name="proj_fox")
    log_f = proj(wt_f, 0, LANES, b_f, tn=LANES, epilogue=_logf_epilogue, out_dtype=F32, name="proj_logf")
    sb_qkv = proj(wt, r_sb, 3 * WIDTH, sb_scale, tn=PROJ_TILE_N, epilogue=_scale_epilogue,
                  out_dtype=BF16, name="proj_sb")
    gates = _proj(n1, wt, r_gate, 2 * D_MODEL, None, rows=n_real, tm=ROW_TILE_REAL,
                  tn=PROJ_TILE_N, epilogue=_gate_epilogue, out_dtype=BF16, name="proj_gates")

    qc, kc = _decay(log_f, n_batch=n_batch, seq=seq)
    o_fox, o_sb = _attention(fox_qkv, qc, kc, sb_qkv, n_batch=n_batch, seq=seq)

    merged = _mix_merge(o_fox, o_sb, gates, w_branch_fox[0], w_branch_sb[0],
                        rows=n_real, tm=ROW_TILE_REAL, tn=PROJ_TILE_N)
    h2 = _mix_proj(merged, w_out[0], h1, rows=n_real, tm=ROW_TILE_REAL, tn=PROJ_TILE_N)
    h3 = _ffn(h2, None, ffn2_norm[0], ffn2_w_gate[0], ffn2_w_up[0], ffn2_w_down[0],
              rows=n_real, tm=ROW_TILE_REAL)
    return h3.reshape(n_batch, seq, D_MODEL)
```

```python
import functools
import math

import jax
import jax.numpy as jnp
from jax import lax
from jax.experimental import pallas as pl
from jax.experimental.pallas import tpu as pltpu

F32 = jnp.float32
BF16 = jnp.bfloat16

D_MODEL = 2048
D_FF = 5632
N_META = 16
HEAD_DIM = 128
N_HEADS = 8
WIDTH = N_HEADS * HEAD_DIM
RMS_EPS = 1e-6
FFN_RESIDUAL_WEIGHT = 0.5
SCALE = HEAD_DIM ** -0.5
LOG2E = math.log2(math.e)

LANES = 128
SUBLANES = 8
BF16_ROWS = 16
VMEM_LIMIT = 58 * 2**20

ROW_TILE_ALL = 912
ROW_TILE_REAL = 1024
FFN_TILE_F = 256
PROJ_TILE_N = 1024
ATT_SUB = 256
ATT_NSUB = 4
ATT_TQ = ATT_SUB * ATT_NSUB
DECAY_TERMS = 3
SUFFIX_TERMS = 1
MAX_EXP2 = 126.0


def _mm(a, b):
    return jnp.dot(a, b, preferred_element_type=F32)


def _mm_nt(a, b):
    return lax.dot_general(a, b, (((1,), (1,)), ((), ())), preferred_element_type=F32)


def _split_bf16(x, n):
    parts = [x.astype(BF16)]
    for _ in range(n - 1):
        x = x - parts[-1].astype(F32)
        parts.append(x.astype(BF16))
    return parts


def _rmsnorm_rows(h, gain):
    ms = jnp.mean(h * h, axis=-1, keepdims=True)
    return h * lax.rsqrt(ms + RMS_EPS) * gain


def _log_sigmoid(z):
    return jnp.minimum(z, 0.0) - jnp.log(1.0 + jnp.exp(-jnp.abs(z)))


def _params(n_grid_axes):
    return pltpu.CompilerParams(dimension_semantics=("arbitrary",) * n_grid_axes,
                                vmem_limit_bytes=VMEM_LIMIT)


def _ffn_kernel(*refs, meta_row, emit_norm):
    refs = list(refs)
    x_ref = refs.pop(0)
    meta_ref = refs.pop(0) if meta_row is not None else None
    gain_ref, wg_ref, wu_ref, wd_ref = refs[:4]
    refs = refs[4:]
    if emit_norm:
        ngain_ref, wf_ref, bf_ref, o_ref, n_ref, lf_ref, xn_ref = refs
    else:
        o_ref, xn_ref = refs
    i = pl.program_id(0)
    j = pl.program_id(1)

    @pl.when(j == 0)
    def _():
        if meta_row is None:
            o_ref[...] = x_ref[...]
        else:
            last = pl.num_programs(0) - 1

            @pl.when(i < last)
            def _():
                o_ref[...] = x_ref[...]

            @pl.when(i == last)
            def _():
                o_ref[0:meta_row, :] = x_ref[0:meta_row, :]
                o_ref[meta_row:meta_row + N_META, :] = meta_ref[...]
        xn_ref[...] = _rmsnorm_rows(o_ref[...], gain_ref[...]).astype(BF16)

    xn = xn_ref[...]
    g = _mm(xn, wg_ref[...].astype(BF16))
    u = _mm(xn, wu_ref[...].astype(BF16))
    a = (g * jax.nn.sigmoid(g)) * (u * FFN_RESIDUAL_WEIGHT)
    o_ref[...] += _mm(a.astype(BF16), wd_ref[...].astype(BF16))

    if emit_norm:
        @pl.when(j == pl.num_programs(1) - 1)
        def _():
            n = _rmsnorm_rows(o_ref[...], ngain_ref[...]).astype(BF16)
            n_ref[...] = n
            lf_ref[...] = _log_sigmoid(_mm(n, wf_ref[...].astype(BF16)) + bf_ref[...]) * LOG2E


def _ffn(x, meta, gain, wg, wu, wd, *, rows, tm, next_gain=None, forget=None):
    ni = rows // tm
    nj = D_FF // FFN_TILE_F
    emit_norm = next_gain is not None
    in_specs = [pl.BlockSpec((tm, D_MODEL), lambda i, j: (i, 0))]
    args = [x]
    meta_row = None
    if meta is not None:
        meta_row = x.shape[0] - (ni - 1) * tm
        assert meta_row + N_META == tm
        in_specs.append(pl.BlockSpec((N_META, D_MODEL), lambda i, j: (0, 0)))
        args.append(meta)
    in_specs += [
        pl.BlockSpec((1, D_MODEL), lambda i, j: (0, 0)),
        pl.BlockSpec((D_MODEL, FFN_TILE_F), lambda i, j: (0, j)),
        pl.BlockSpec((D_MODEL, FFN_TILE_F), lambda i, j: (0, j)),
        pl.BlockSpec((FFN_TILE_F, D_MODEL), lambda i, j: (j, 0)),
    ]
    args += [gain.reshape(1, D_MODEL), wg, wu, wd]
    out_shape = [jax.ShapeDtypeStruct((rows, D_MODEL), F32)]
    out_specs = [pl.BlockSpec((tm, D_MODEL), lambda i, j: (i, 0))]
    if emit_norm:
        w_f, b_f = forget
        in_specs += [pl.BlockSpec((1, D_MODEL), lambda i, j: (0, 0)),
                     pl.BlockSpec((D_MODEL, LANES), lambda i, j: (0, 0)),
                     pl.BlockSpec((1, LANES), lambda i, j: (0, 0))]
        args += [next_gain.reshape(1, D_MODEL), w_f, b_f]
        out_shape += [jax.ShapeDtypeStruct((rows, D_MODEL), BF16), jax.ShapeDtypeStruct((rows, LANES), F32)]
        out_specs += [pl.BlockSpec((tm, D_MODEL), lambda i, j: (i, 0)), pl.BlockSpec((tm, LANES), lambda i, j: (i, 0))]
    res = pl.pallas_call(
        functools.partial(_ffn_kernel, meta_row=meta_row, emit_norm=emit_norm),
        grid=(ni, nj),
        in_specs=in_specs,
        out_specs=out_specs,
        out_shape=out_shape,
        scratch_shapes=[pltpu.VMEM((tm, D_MODEL), BF16)],
        compiler_params=_params(2),
        name="ffn_norm" if emit_norm else "ffn",
    )(*args)
    return res if emit_norm else res[0]


def _proj_kernel(n_ref, w_ref, *refs, epilogue):
    p_ref = refs[0] if len(refs) == 3 else None
    o_ref, wb_ref = refs[-2:]

    @pl.when(pl.program_id(1) == 0)
    def _():
        wb_ref[...] = w_ref[...].T.astype(BF16)

    def run(fn):
        for rows in _row_chunks(n_ref.shape[0]):
            o_ref[rows, :] = fn(_mm(n_ref[rows, :], wb_ref[...])).astype(o_ref.dtype)

    epilogue(run, p_ref, pl.program_id(0))


def _row_chunks(tm):
    n = next(n for n in (4, 3, 2, 1) if tm % (n * BF16_ROWS) == 0)
    return [slice(c * (tm // n), (c + 1) * (tm // n)) for c in range(n)]


def _fox_epilogue(run, p_ref, j):
    def head_norm(y):
        heads = [slice(h * HEAD_DIM, (h + 1) * HEAD_DIM) for h in range(y.shape[1] // HEAD_DIM)]
        return jnp.concatenate([_rmsnorm_rows(y[:, sl], p_ref[:, sl]) for sl in heads], axis=1)

    pl.when(j < 2)(lambda: run(head_norm))
    pl.when(j >= 2)(lambda: run(lambda y: y))


def _scale_epilogue(run, p_ref, j):
    run(lambda y: y * p_ref[...])


def _gate_epilogue(run, p_ref, j):
    run(jax.nn.sigmoid)


def _proj(n, wt, row0, n_cols, p, *, rows, tm, tn, epilogue, out_dtype, name):
    assert n_cols % tn == 0
    nj, ni = n_cols // tn, rows // tm
    if row0 % tn == 0:
        w_spec = pl.BlockSpec((tn, D_MODEL), lambda j, i: (row0 // tn + j, 0))
    else:
        assert row0 % SUBLANES == 0 and tn % SUBLANES == 0
        w_spec = pl.BlockSpec((pl.Element(tn), pl.Element(D_MODEL)),
                              lambda j, i: ((row0 // SUBLANES + j * (tn // SUBLANES)) * SUBLANES, 0))
    in_specs = [pl.BlockSpec((tm, D_MODEL), lambda j, i: (i, 0)), w_spec]
    args = [n, wt]
    if p is not None:
        in_specs.append(pl.BlockSpec((1, tn), lambda j, i: (0, j)))
        args.append(p)
    return pl.pallas_call(
        functools.partial(_proj_kernel, epilogue=epilogue),
        grid=(nj, ni),
        in_specs=in_specs,
        out_specs=pl.BlockSpec((tm, tn), lambda j, i: (i, j)),
        out_shape=jax.ShapeDtypeStruct((rows, n_cols), out_dtype),
        scratch_shapes=[pltpu.VMEM((D_MODEL, tn), BF16)],
        compiler_params=_params(2),
        name=name,
    )(*args)


def _decay_kernel(lf_ref, qc_ref, kc_ref, c_ref, *, n_batch, seq):
    T = DECAY_TERMS
    n_real = n_batch * seq
    r_i = lax.broadcasted_iota(jnp.int32, (LANES, LANES), 0)
    c_i = lax.broadcasted_iota(jnp.int32, (LANES, LANES), 1)
    tri = jnp.where(r_i >= c_i, 1.0, 0.0).astype(BF16)

    def cum_block(x, carry):
        acc = carry
        for p in _split_bf16(x, T):
            acc = acc + _mm(tri, p)
        return acc

    xm = jnp.concatenate([lf_ref[n_real:n_real + N_META, :], jnp.zeros((LANES - N_META, LANES), F32)], axis=0)
    cm = cum_block(xm, jnp.zeros((1, LANES), F32))
    c_ref[n_real:n_real + N_META, :] = cm[:N_META, :]
    base = cm[N_META - 1:N_META, :]

    def body(t, carries):
        out = []
        for b, carry in enumerate(carries):
            r0 = pl.multiple_of(b * seq + t * LANES, LANES)
            c = cum_block(lf_ref[pl.ds(r0, LANES), :], carry)
            c_ref[pl.ds(r0, LANES), :] = c
            out.append(c[LANES - 1:LANES, :])
        return tuple(out)
    lax.fori_loop(0, seq // LANES, body, (base,) * n_batch)

    k_i = lax.broadcasted_iota(jnp.int32, (T * LANES, LANES), 0)
    n_i = lax.broadcasted_iota(jnp.int32, (T * LANES, LANES), 1)
    sel_q = jnp.zeros((T * LANES, LANES), F32)
    sel_k = jnp.zeros((T * LANES, LANES), F32)
    lane = lax.broadcasted_iota(jnp.int32, (1, LANES), 1)
    ones_q = jnp.zeros((1, LANES), F32)
    ones_k = jnp.zeros((1, LANES), F32)
    for h in range(N_HEADS):
        for p in range(T):
            row = p * LANES + h
            sel_q = jnp.where((k_i == row) & (n_i == 2 * T * h + p), 1.0, sel_q)
            sel_k = jnp.where((k_i == row) & (n_i == 2 * T * h + T + p), -1.0, sel_k)
            ones_q = jnp.where(lane == 2 * T * h + T + p, 1.0, ones_q)
            ones_k = jnp.where(lane == 2 * T * h + p, 1.0, ones_k)
    sel_q = sel_q.astype(BF16)
    sel_k = sel_k.astype(BF16)

    def expand(c):
        pieces = jnp.concatenate(_split_bf16(c, T), axis=1)
        return ((_mm(pieces, sel_q) + ones_q).astype(BF16), (_mm(pieces, sel_k) + ones_k).astype(BF16))

    chunk = 1024
    def ebody(t, _):
        r0 = pl.multiple_of(t * chunk, chunk)
        q, k = expand(c_ref[pl.ds(r0, chunk), :])
        qc_ref[pl.ds(r0, chunk), :] = q
        kc_ref[pl.ds(r0, chunk), :] = k
        return 0
    lax.fori_loop(0, n_real // chunk, ebody, 0)
    _, km = expand(c_ref[n_real:n_real + N_META, :])
    kc_ref[n_real:n_real + N_META, :] = km


def _decay(lf, *, n_batch, seq):
    n_real = n_batch * seq
    return pl.pallas_call(
        functools.partial(_decay_kernel, n_batch=n_batch, seq=seq),
        out_shape=[jax.ShapeDtypeStruct((n_real, LANES), BF16),
                   jax.ShapeDtypeStruct((n_real + N_META, LANES), BF16)],
        scratch_shapes=[pltpu.VMEM((n_real + N_META, LANES), F32)],
        compiler_params=pltpu.CompilerParams(vmem_limit_bytes=VMEM_LIMIT),
        name="decay",
    )(lf)


def _eye(n):
    return jnp.where(lax.broadcasted_iota(jnp.int32, (n, n), 0) == lax.broadcasted_iota(jnp.int32, (n, n), 1),
                     1.0, 0.0).astype(BF16)


def _stage_keys_values(kpad_ref, vt_ref, meta_k_refs, v_ref, mv_ref, *, seq):
    eye = _eye(HEAD_DIM)
    kpad_ref[...] = jnp.zeros_like(kpad_ref)
    for n, ref in enumerate(meta_k_refs):
        kpad_ref[0:N_META, n * HEAD_DIM:(n + 1) * HEAD_DIM] = ref[...]
    vpad = jnp.concatenate([mv_ref[...], jnp.zeros((LANES - N_META, HEAD_DIM), BF16)], axis=0)
    vt_ref[:, seq:seq + LANES] = _mm_nt(eye, vpad).astype(BF16)
    chunk = 512
    for c0 in range(0, seq, chunk):
        vt_ref[:, c0:c0 + chunk] = _mm_nt(eye, v_ref[c0:c0 + chunk, :]).astype(BF16)


def _store_transposed(o_ref, r0, acc_t):
    o_ref[pl.ds(r0, ATT_SUB), :] = acc_t.T.astype(o_ref.dtype)


def _fox_head(q_ref, qc_ref, k_ref, kc_ref, v_ref, mk_ref, mkc_ref, mv_ref, o_ref, kpad_ref, vt_ref,
              sa_ref, sb_ref, *, seq):
    T = DECAY_TERMS
    h = pl.program_id(1)
    _stage_keys_values(kpad_ref, vt_ref, (mk_ref, mkc_ref), v_ref, mv_ref, seq=seq)

    lane = lax.broadcasted_iota(jnp.int32, (1, LANES), 1)
    head_lanes = (lane >= 2 * T * h) & (lane < 2 * T * (h + 1))
    meta_valid = lax.broadcasted_iota(jnp.int32, (LANES, ATT_SUB), 0) < N_META
    causal = (lax.broadcasted_iota(jnp.int32, (ATT_SUB, ATT_SUB), 0)
              <= lax.broadcasted_iota(jnp.int32, (ATT_SUB, ATT_SUB), 1))
    all_subs = tuple(range(ATT_NSUB))

    def keys(c0):
        return jnp.concatenate([k_ref[pl.ds(c0, ATT_SUB), :], kc_ref[pl.ds(c0, ATT_SUB), :]], axis=1)

    def online(s, vt, m, l, acc):
        m_new = jnp.maximum(m, jnp.max(s, axis=0, keepdims=True))
        alpha = jnp.exp2(m - m_new)
        p = jnp.exp2(s - m_new)
        l = alpha * l + jnp.sum(p, axis=0, keepdims=True)
        acc = alpha * acc + _mm(vt, p.astype(BF16))
        return m_new, l, acc

    def block(i):
        r0 = pl.multiple_of(i * ATT_TQ, ATT_TQ)
        qx = jnp.where(head_lanes, qc_ref[pl.ds(r0, ATT_TQ), :], jnp.zeros((), BF16))
        qa = jnp.concatenate([q_ref[pl.ds(r0, ATT_TQ), :], qx], axis=1)
        subs = [qa[n * ATT_SUB:(n + 1) * ATT_SUB] for n in all_subs]

        def issue(buf_ref, c0, which=all_subs):
            ka = keys(c0)
            for n in which:
                buf_ref[n] = _mm_nt(ka, subs[n])

        def consume(buf_ref, c0, state, masked=None, which=all_subs):
            vt = vt_ref[:, pl.ds(c0, ATT_SUB)]
            out = list(state)
            for n in which:
                s = buf_ref[n]
                if n == masked:
                    s = jnp.where(causal, s, -jnp.inf)
                out[n] = online(s, vt, *out[n])
            return tuple(out)

        issue(sa_ref, 0)

        state = []
        for qs in subs:
            s = jnp.where(meta_valid, _mm_nt(kpad_ref[...], qs), -jnp.inf)
            m = jnp.max(s, axis=0, keepdims=True)
            p = jnp.exp2(s - m)
            state.append((m, jnp.sum(p, axis=0, keepdims=True), _mm(vt_ref[:, seq:seq + LANES], p.astype(BF16))))

        def kv_pair(t, state):
            c0 = pl.multiple_of(t * (2 * ATT_SUB), 2 * ATT_SUB)
            c1 = pl.multiple_of(c0 + ATT_SUB, ATT_SUB)
            issue(sb_ref, c1)
            state = consume(sa_ref, c0, state)
            issue(sa_ref, pl.multiple_of(c0 + 2 * ATT_SUB, 2 * ATT_SUB))
            return consume(sb_ref, c1, state)

        def finish(state):
            bufs = (sa_ref, sb_ref)
            for kb in all_subs:
                if kb + 1 < ATT_NSUB:
                    issue(bufs[(kb + 1) % 2], pl.multiple_of(r0 + (kb + 1) * ATT_SUB, ATT_SUB), all_subs[kb + 1:])
                state = consume(bufs[kb % 2], pl.multiple_of(r0 + kb * ATT_SUB, ATT_SUB), state, kb, all_subs[kb:])
            for n, (m, l, acc) in enumerate(state):
                _store_transposed(o_ref, pl.multiple_of(r0 + n * ATT_SUB, ATT_SUB), acc / l)

        return tuple(state), kv_pair, finish

    return block


def _sb_head(q_ref, k_ref, v_ref, mk_ref, mv_ref, o_ref, kpad_ref, vt_ref, z_ref, *, seq):
    _stage_keys_values(kpad_ref, vt_ref, (mk_ref,), v_ref, mv_ref, seq=seq)

    def from_here_on(n):
        return jnp.where(lax.broadcasted_iota(jnp.int32, (n, n), 0) <= lax.broadcasted_iota(jnp.int32, (n, n), 1),
                         1.0, 0.0).astype(BF16)

    u_blk = from_here_on(ATT_SUB)
    u_meta = from_here_on(LANES)
    meta_valid = lax.broadcasted_iota(jnp.int32, (LANES, ATT_SUB), 0) < N_META
    strict = (lax.broadcasted_iota(jnp.int32, (ATT_SUB, ATT_SUB), 0)
              < lax.broadcasted_iota(jnp.int32, (ATT_SUB, ATT_SUB), 1))
    all_subs = tuple(range(ATT_NSUB))

    def steps(items, states, zs=None):
        states = list(states)
        if zs is None:
            zs = [_mm_nt(kblk, qh) for _, qh, kblk, _, _, _ in items]
        cums = []
        for z, (_, _, _, _, u, mask) in zip(zs, items):
            sp = jnp.maximum(jnp.log2(1.0 + jnp.exp2(jnp.minimum(z, MAX_EXP2))), z)
            if mask is not None:
                sp = jnp.where(mask, sp, 0.0)
            pieces = _split_bf16(sp, SUFFIX_TERMS)
            cums.append(functools.reduce(lambda a, b: a + b, [_mm(u, p) for p in pieces]))
        for z, cum, (half, _, _, vt, _, mask) in zip(zs, cums, items):
            r, acc = states[half]
            w = jnp.exp2(z - cum - r)
            if mask is not None:
                w = jnp.where(mask, w, 0.0)
            states[half] = (r + cum[0:1, :], acc + _mm(vt, w.astype(BF16)))
        return tuple(states)

    def block(i):
        r0 = pl.multiple_of(i * ATT_TQ, ATT_TQ)
        q = [q_ref[pl.ds(pl.multiple_of(r0 + n * ATT_SUB, ATT_SUB), ATT_SUB), :] for n in all_subs]
        zero = (jnp.zeros((1, ATT_SUB), F32), jnp.zeros((HEAD_DIM, ATT_SUB), F32))

        def item(n, c0, mask):
            return (n, q[n], k_ref[pl.ds(c0, ATT_SUB), :], vt_ref[:, pl.ds(c0, ATT_SUB)], u_blk, mask)

        def pair_items(jj):
            items = []
            for d in (1, 2):
                c0 = pl.multiple_of(jnp.maximum(r0 - (2 * jj + d) * ATT_SUB, 0), ATT_SUB)
                items += [item(n, c0, None) for n in all_subs]
            return items

        def scores(items):
            return [_mm_nt(kblk, qs) for _, qs, kblk, _, _, _ in items]

        for n, z in enumerate(scores(pair_items(0))):
            z_ref[n] = z

        diag = []
        for kb in reversed(all_subs):
            c0 = pl.multiple_of(r0 + kb * ATT_SUB, ATT_SUB)
            diag += [item(n, c0, strict if n == kb else None) for n in all_subs[kb:]]
        states = steps(diag, (zero,) * ATT_NSUB)

        def kv_pair(jj, states):
            zs = [z_ref[n] for n in range(2 * ATT_NSUB)]
            ahead = scores(pair_items(jj + 1))
            states = steps(pair_items(jj), states, zs)
            for n, z in enumerate(ahead):
                z_ref[n] = z
            return states

        def finish(states):
            meta = [(n, q[n], kpad_ref[...], vt_ref[:, seq:seq + LANES], u_meta, meta_valid) for n in all_subs]
            for n, (_, acc) in enumerate(steps(meta, states)):
                _store_transposed(o_ref, pl.multiple_of(r0 + n * ATT_SUB, ATT_SUB), acc)

        return states, kv_pair, finish

    return block


def _attention_kernel(fq, fqc, fk, fkc, fv, fmk, fmkc, fmv, sq, sk, sv, smk, smv, of_ref, os_ref,
                      f_kpad, f_vt, f_sa, f_sb, s_kpad, s_vt, s_z, *, seq):
    fox = _fox_head(fq, fqc, fk, fkc, fv, fmk, fmkc, fmv, of_ref, f_kpad, f_vt, f_sa, f_sb, seq=seq)
    stick = _sb_head(sq, sk, sv, smk, smv, os_ref, s_kpad, s_vt, s_z, seq=seq)

    def q_block(i, _):
        f_state, f_pair, f_finish = fox(i)
        s_state, s_pair, s_finish = stick(i)

        def kv_pair(t, carry):
            s_next = s_pair(t, carry[1])
            return f_pair(t, carry[0]), s_next

        f_state, s_state = lax.fori_loop(0, i * (ATT_NSUB // 2), kv_pair, (f_state, s_state))
        f_finish(f_state)
        s_finish(s_state)
        return 0

    lax.fori_loop(0, seq // ATT_TQ, q_block, 0)


def _head_spec(rows, col_block0):
    return pl.BlockSpec((rows, HEAD_DIM), lambda b, h: (b, col_block0 + h))


def _meta_spec(n_real, col_block0):
    return pl.BlockSpec((N_META, HEAD_DIM), lambda b, h: (n_real // N_META, col_block0 + h))


def _attention(fox_qkv, qc, kc, sb_qkv, *, n_batch, seq):
    n_real = n_batch * seq
    shared = pl.BlockSpec((seq, LANES), lambda b, h: (b, 0))
    qkv_specs = [_head_spec(seq, 0), _head_spec(seq, N_HEADS), _head_spec(seq, 2 * N_HEADS)]
    meta_specs = [_meta_spec(n_real, N_HEADS), _meta_spec(n_real, 2 * N_HEADS)]
    score_buf = pltpu.VMEM((ATT_NSUB, ATT_SUB, ATT_SUB), F32)
    return pl.pallas_call(
        functools.partial(_attention_kernel, seq=seq),
        grid=(n_batch, N_HEADS),
        in_specs=[qkv_specs[0], shared, qkv_specs[1], shared, qkv_specs[2],
                  meta_specs[0], pl.BlockSpec((N_META, LANES), lambda b, h: (n_real // N_META, 0)), meta_specs[1],
                  *qkv_specs, *meta_specs],
        out_specs=[_head_spec(seq, 0), _head_spec(seq, 0)],
        out_shape=[jax.ShapeDtypeStruct((n_real, WIDTH), BF16)] * 2,
        scratch_shapes=[pltpu.VMEM((LANES, 2 * HEAD_DIM), BF16), pltpu.VMEM((HEAD_DIM, seq + LANES), BF16),
                        score_buf, score_buf,
                        pltpu.VMEM((LANES, HEAD_DIM), BF16), pltpu.VMEM((HEAD_DIM, seq + LANES), BF16),
                        pltpu.VMEM((2 * ATT_NSUB, ATT_SUB, ATT_SUB), F32)],
        compiler_params=_params(2),
        name="attention",
    )(fox_qkv, qc, fox_qkv, kc, fox_qkv, fox_qkv, kc, fox_qkv, sb_qkv, sb_qkv, sb_qkv, sb_qkv, sb_qkv)


def _mix_merge_kernel(of_ref, os_ref, gf_ref, gs_ref, wbf_ref, wbs_ref, o_ref, wbf_b, wbs_b):
    @pl.when(pl.program_id(1) == 0)
    def _():
        wbf_b[...] = wbf_ref[...].astype(BF16)
        wbs_b[...] = wbs_ref[...].astype(BF16)

    for rows in _row_chunks(of_ref.shape[0]):
        tf = _mm(of_ref[rows, :], wbf_b[...])
        ts = _mm(os_ref[rows, :], wbs_b[...])
        merged = gf_ref[rows, :].astype(F32) * tf + gs_ref[rows, :].astype(F32) * ts
        o_ref[rows, :] = merged.astype(o_ref.dtype)


def _mix_merge(o_fox, o_sb, gates, wbf, wbs, *, rows, tm, tn):
    ni, nj = rows // tm, D_MODEL // tn
    return pl.pallas_call(
        _mix_merge_kernel,
        grid=(nj, ni),
        in_specs=[
            pl.BlockSpec((tm, WIDTH), lambda j, i: (i, 0)),
            pl.BlockSpec((tm, WIDTH), lambda j, i: (i, 0)),
            pl.BlockSpec((tm, tn), lambda j, i: (i, j)),
            pl.BlockSpec((tm, tn), lambda j, i: (i, nj + j)),
            pl.BlockSpec((WIDTH, tn), lambda j, i: (0, j)),
            pl.BlockSpec((WIDTH, tn), lambda j, i: (0, j)),
        ],
        out_specs=pl.BlockSpec((tm, tn), lambda j, i: (i, j)),
        out_shape=jax.ShapeDtypeStruct((rows, D_MODEL), BF16),
        scratch_shapes=[pltpu.VMEM((WIDTH, tn), BF16), pltpu.VMEM((WIDTH, tn), BF16)],
        compiler_params=_params(2),
        name="mix_merge",
    )(o_fox, o_sb, gates, gates, wbf, wbs)


def _mix_proj_kernel(m_ref, w_ref, h_ref, o_ref, wb_ref):
    @pl.when(pl.program_id(1) == 0)
    def _():
        wb_ref[...] = w_ref[...].astype(BF16)

    for rows in _row_chunks(m_ref.shape[0]):
        o_ref[rows, :] = h_ref[rows, :] + _mm(m_ref[rows, :], wb_ref[...])


def _mix_proj(merged, wo, h, *, rows, tm, tn):
    ni, nj = rows // tm, D_MODEL // tn
    return pl.pallas_call(
        _mix_proj_kernel,
        grid=(nj, ni),
        in_specs=[
            pl.BlockSpec((tm, D_MODEL), lambda j, i: (i, 0)),
            pl.BlockSpec((D_MODEL, tn), lambda j, i: (0, j)),
            pl.BlockSpec((tm, tn), lambda j, i: (i, j)),
        ],
        out_specs=pl.BlockSpec((tm, tn), lambda j, i: (i, j)),
        out_shape=jax.ShapeDtypeStruct((rows, D_MODEL), F32),
        scratch_shapes=[pltpu.VMEM((D_MODEL, tn), BF16)],
        compiler_params=_params(2),
        name="mix_proj",
    )(merged, wo, h)


def kernel(x, meta_tokens, ffn1_norm, ffn1_w_gate, ffn1_w_up, ffn1_w_down, mix_norm, w_in, b_forget, fox_q_norm, fox_k_norm, w_branch_fox, w_branch_sb, w_out, ffn2_norm, ffn2_w_gate, ffn2_w_up, ffn2_w_down):
    n_batch, seq, _ = x.shape
    n_real = n_batch * seq
    n_all = n_real + N_META
    assert ffn1_norm.shape[0] == 1
    assert n_all % ROW_TILE_ALL == 0 and n_real % ROW_TILE_REAL == 0 and seq % ATT_TQ == 0

    wt = w_in[0].T
    r_f = 3 * WIDTH
    r_sb = r_f + N_HEADS
    r_gate = r_sb + 3 * WIDTH
    w_f = jnp.pad(w_in[0][:, r_f:r_sb], ((0, 0), (0, LANES - N_HEADS)))
    b_f = jnp.pad(b_forget[0].reshape(1, N_HEADS), ((0, 0), (0, LANES - N_HEADS)))
    fox_gain = jnp.concatenate([fox_q_norm[0].reshape(1, WIDTH) * (SCALE * LOG2E), fox_k_norm[0].reshape(1, WIDTH),
                                jnp.ones((1, WIDTH), F32)], axis=1)
    sb_scale = jnp.concatenate([jnp.full((1, WIDTH), SCALE * LOG2E, F32), jnp.ones((1, 2 * WIDTH), F32)], axis=1)

    h1, n1, log_f = _ffn(x.reshape(n_real, D_MODEL), meta_tokens.astype(F32), ffn1_norm[0], ffn1_w_gate[0],
                         ffn1_w_up[0], ffn1_w_down[0], rows=n_all, tm=ROW_TILE_ALL, next_gain=mix_norm[0],
                         forget=(w_f, b_f))

    proj = functools.partial(_proj, n1, rows=n_all, tm=ROW_TILE_ALL)
    fox_qkv = proj(wt, 0, 3 * WIDTH, fox_gain, tn=PROJ_TILE_N, epilogue=_fox_epilogue,
                   out_dtype=BF16, name="proj_fox")
    sb_qkv = proj(wt, r_sb, 3 * WIDTH, sb_scale, tn=PROJ_TILE_N, epilogue=_scale_epilogue,
                  out_dtype=BF16, name="proj_sb")
    gates = _proj(n1, wt, r_gate, 2 * D_MODEL, None, rows=n_real, tm=ROW_TILE_REAL,
                  tn=PROJ_TILE_N, epilogue=_gate_epilogue, out_dtype=BF16, name="proj_gates")

    qc, kc = _decay(log_f, n_batch=n_batch, seq=seq)
    o_fox, o_sb = _attention(fox_qkv, qc, kc, sb_qkv, n_batch=n_batch, seq=seq)

    merged = _mix_merge(o_fox, o_sb, gates, w_branch_fox[0], w_branch_sb[0],
                        rows=n_real, tm=ROW_TILE_REAL, tn=PROJ_TILE_N)
    h2 = _mix_proj(merged, w_out[0], h1, rows=n_real, tm=ROW_TILE_REAL, tn=PROJ_TILE_N)
    h3 = _ffn(h2, None, ffn2_norm[0], ffn2_w_gate[0], ffn2_w_up[0], ffn2_w_down[0],
              rows=n_real, tm=ROW_TILE_REAL)
    return h3.reshape(n_batch, seq, D_MODEL)
```

```python
import functools
import math

import jax
import jax.numpy as jnp
from jax import lax
from jax.experimental import pallas as pl
from jax.experimental.pallas import tpu as pltpu

F32 = jnp.float32
BF16 = jnp.bfloat16

D_MODEL = 2048
D_FF = 5632
N_META = 16
HEAD_DIM = 128
N_HEADS = 8
WIDTH = N_HEADS * HEAD_DIM
RMS_EPS = 1e-6
FFN_RESIDUAL_WEIGHT = 0.5
SCALE = HEAD_DIM ** -0.5
LOG2E = math.log2(math.e)

LANES = 128
SUBLANES = 8
BF16_ROWS = 16
VMEM_LIMIT = 58 * 2**20

ROW_TILE_ALL = 912
ROW_TILE_REAL = 1024
FFN_TILE_F = 256
PROJ_TILE_N = 1024
ATT_SUB = 256
ATT_NSUB = 4
ATT_TQ = ATT_SUB * ATT_NSUB
DECAY_TERMS = 3
SUFFIX_TERMS = 1
MAX_EXP2 = 126.0


def _mm(a, b):
    return jnp.dot(a, b, preferred_element_type=F32)


def _mm_nt(a, b):
    return lax.dot_general(a, b, (((1,), (1,)), ((), ())), preferred_element_type=F32)


def _split_bf16(x, n):
    parts = [x.astype(BF16)]
    for _ in range(n - 1):
        x = x - parts[-1].astype(F32)
        parts.append(x.astype(BF16))
    return parts


def _rmsnorm_rows(h, gain):
    ms = jnp.mean(h * h, axis=-1, keepdims=True)
    return h * lax.rsqrt(ms + RMS_EPS) * gain


def _log_sigmoid(z):
    return jnp.minimum(z, 0.0) - jnp.log(1.0 + jnp.exp(-jnp.abs(z)))


def _params(n_grid_axes):
    return pltpu.CompilerParams(dimension_semantics=("arbitrary",) * n_grid_axes,
                                vmem_limit_bytes=VMEM_LIMIT)


def _ffn_kernel(*refs, meta_row, emit_norm):
    refs = list(refs)
    x_ref = refs.pop(0)
    meta_ref = refs.pop(0) if meta_row is not None else None
    gain_ref, wg_ref, wu_ref, wd_ref = refs[:4]
    refs = refs[4:]
    if emit_norm:
        ngain_ref, wf_ref, bf_ref, o_ref, n_ref, lf_ref, xn_ref = refs
    else:
        o_ref, xn_ref = refs
    i = pl.program_id(0)
    j = pl.program_id(1)

    @pl.when(j == 0)
    def _():
        if meta_row is None:
            o_ref[...] = x_ref[...]
        else:
            last = pl.num_programs(0) - 1

            @pl.when(i < last)
            def _():
                o_ref[...] = x_ref[...]

            @pl.when(i == last)
            def _():
                o_ref[0:meta_row, :] = x_ref[0:meta_row, :]
                o_ref[meta_row:meta_row + N_META, :] = meta_ref[...]
        xn_ref[...] = _rmsnorm_rows(o_ref[...], gain_ref[...]).astype(BF16)

    xn = xn_ref[...]
    g = _mm(xn, wg_ref[...].astype(BF16))
    u = _mm(xn, wu_ref[...].astype(BF16))
    a = (g * jax.nn.sigmoid(g)) * (u * FFN_RESIDUAL_WEIGHT)
    o_ref[...] += _mm(a.astype(BF16), wd_ref[...].astype(BF16))

    if emit_norm:
        @pl.when(j == pl.num_programs(1) - 1)
        def _():
            n = _rmsnorm_rows(o_ref[...], ngain_ref[...]).astype(BF16)
            n_ref[...] = n
            lf_ref[...] = _log_sigmoid(_mm(n, wf_ref[...].astype(BF16)) + bf_ref[...]) * LOG2E


def _ffn(x, meta, gain, wg, wu, wd, *, rows, tm, next_gain=None, forget=None):
    ni = rows // tm
    nj = D_FF // FFN_TILE_F
    emit_norm = next_gain is not None
    in_specs = [pl.BlockSpec((tm, D_MODEL), lambda i, j: (i, 0))]
    args = [x]
    meta_row = None
    if meta is not None:
        meta_row = x.shape[0] - (ni - 1) * tm
        assert meta_row + N_META == tm
        in_specs.append(pl.BlockSpec((N_META, D_MODEL), lambda i, j: (0, 0)))
        args.append(meta)
    in_specs += [
        pl.BlockSpec((1, D_MODEL), lambda i, j: (0, 0)),
        pl.BlockSpec((D_MODEL, FFN_TILE_F), lambda i, j: (0, j)),
        pl.BlockSpec((D_MODEL, FFN_TILE_F), lambda i, j: (0, j)),
        pl.BlockSpec((FFN_TILE_F, D_MODEL), lambda i, j: (j, 0)),
    ]
    args += [gain.reshape(1, D_MODEL), wg, wu, wd]
    out_shape = [jax.ShapeDtypeStruct((rows, D_MODEL), F32)]
    out_specs = [pl.BlockSpec((tm, D_MODEL), lambda i, j: (i, 0))]
    if emit_norm:
        w_f, b_f = forget
        in_specs += [pl.BlockSpec((1, D_MODEL), lambda i, j: (0, 0)),
                     pl.BlockSpec((D_MODEL, LANES), lambda i, j: (0, 0)),
                     pl.BlockSpec((1, LANES), lambda i, j: (0, 0))]
        args += [next_gain.reshape(1, D_MODEL), w_f, b_f]
        out_shape += [jax.ShapeDtypeStruct((rows, D_MODEL), BF16), jax.ShapeDtypeStruct((rows, LANES), F32)]
        out_specs += [pl.BlockSpec((tm, D_MODEL), lambda i, j: (i, 0)), pl.BlockSpec((tm, LANES), lambda i, j: (i, 0))]
    res = pl.pallas_call(
        functools.partial(_ffn_kernel, meta_row=meta_row, emit_norm=emit_norm),
        grid=(ni, nj),
        in_specs=in_specs,
        out_specs=out_specs,
        out_shape=out_shape,
        scratch_shapes=[pltpu.VMEM((tm, D_MODEL), BF16)],
        compiler_params=_params(2),
        name="ffn_norm" if emit_norm else "ffn",
    )(*args)
    return res if emit_norm else res[0]


def _proj_kernel(n_ref, w_ref, *refs, epilogue):
    p_ref = refs[0] if len(refs) == 3 else None
    o_ref, wb_ref = refs[-2:]

    @pl.when(pl.program_id(1) == 0)
    def _():
        wb_ref[...] = w_ref[...].T.astype(BF16)

    def run(fn):
        for rows in _row_chunks(n_ref.shape[0]):
            o_ref[rows, :] = fn(_mm(n_ref[rows, :], wb_ref[...])).astype(o_ref.dtype)

    epilogue(run, p_ref, pl.program_id(0))


def _row_chunks(tm):
    n = next(n for n in (4, 3, 2, 1) if tm % (n * BF16_ROWS) == 0)
    return [slice(c * (tm // n), (c + 1) * (tm // n)) for c in range(n)]


def _fox_epilogue(run, p_ref, j):
    def head_norm(y):
        heads = [slice(h * HEAD_DIM, (h + 1) * HEAD_DIM) for h in range(y.shape[1] // HEAD_DIM)]
        return jnp.concatenate([_rmsnorm_rows(y[:, sl], p_ref[:, sl]) for sl in heads], axis=1)

    pl.when(j < 2)(lambda: run(head_norm))
    pl.when(j >= 2)(lambda: run(lambda y: y))


def _scale_epilogue(run, p_ref, j):
    run(lambda y: y * p_ref[...])


def _gate_epilogue(run, p_ref, j):
    run(jax.nn.sigmoid)


def _proj(n, wt, row0, n_cols, p, *, rows, tm, tn, epilogue, out_dtype, name):
    assert n_cols % tn == 0
    nj, ni = n_cols // tn, rows // tm
    if row0 % tn == 0:
        w_spec = pl.BlockSpec((tn, D_MODEL), lambda j, i: (row0 // tn + j, 0))
    else:
        assert row0 % SUBLANES == 0 and tn % SUBLANES == 0
        w_spec = pl.BlockSpec((pl.Element(tn), pl.Element(D_MODEL)),
                              lambda j, i: ((row0 // SUBLANES + j * (tn // SUBLANES)) * SUBLANES, 0))
    in_specs = [pl.BlockSpec((tm, D_MODEL), lambda j, i: (i, 0)), w_spec]
    args = [n, wt]
    if p is not None:
        in_specs.append(pl.BlockSpec((1, tn), lambda j, i: (0, j)))
        args.append(p)
    return pl.pallas_call(
        functools.partial(_proj_kernel, epilogue=epilogue),
        grid=(nj, ni),
        in_specs=in_specs,
        out_specs=pl.BlockSpec((tm, tn), lambda j, i: (i, j)),
        out_shape=jax.ShapeDtypeStruct((rows, n_cols), out_dtype),
        scratch_shapes=[pltpu.VMEM((D_MODEL, tn), BF16)],
        compiler_params=_params(2),
        name=name,
    )(*args)


def _decay_kernel(lf_ref, qc_ref, kc_ref, c_ref, *, n_batch, seq):
    T = DECAY_TERMS
    n_real = n_batch * seq
    r_i = lax.broadcasted_iota(jnp.int32, (LANES, LANES), 0)
    c_i = lax.broadcasted_iota(jnp.int32, (LANES, LANES), 1)
    tri = jnp.where(r_i >= c_i, 1.0, 0.0).astype(BF16)

    def cum_block(x, carry):
        acc = carry
        for p in _split_bf16(x, T):
            acc = acc + _mm(tri, p)
        return acc

    xm = jnp.concatenate([lf_ref[n_real:n_real + N_META, :], jnp.zeros((LANES - N_META, LANES), F32)], axis=0)
    cm = cum_block(xm, jnp.zeros((1, LANES), F32))
    c_ref[n_real:n_real + N_META, :] = cm[:N_META, :]
    base = cm[N_META - 1:N_META, :]

    def body(t, carries):
        out = []
        for b, carry in enumerate(carries):
            r0 = pl.multiple_of(b * seq + t * LANES, LANES)
            c = cum_block(lf_ref[pl.ds(r0, LANES), :], carry)
            c_ref[pl.ds(r0, LANES), :] = c
            out.append(c[LANES - 1:LANES, :])
        return tuple(out)
    lax.fori_loop(0, seq // LANES, body, (base,) * n_batch)

    k_i = lax.broadcasted_iota(jnp.int32, (T * LANES, LANES), 0)
    n_i = lax.broadcasted_iota(jnp.int32, (T * LANES, LANES), 1)
    sel_q = jnp.zeros((T * LANES, LANES), F32)
    sel_k = jnp.zeros((T * LANES, LANES), F32)
    lane = lax.broadcasted_iota(jnp.int32, (1, LANES), 1)
    ones_q = jnp.zeros((1, LANES), F32)
    ones_k = jnp.zeros((1, LANES), F32)
    for h in range(N_HEADS):
        for p in range(T):
            row = p * LANES + h
            sel_q = jnp.where((k_i == row) & (n_i == 2 * T * h + p), 1.0, sel_q)
            sel_k = jnp.where((k_i == row) & (n_i == 2 * T * h + T + p), -1.0, sel_k)
            ones_q = jnp.where(lane == 2 * T * h + T + p, 1.0, ones_q)
            ones_k = jnp.where(lane == 2 * T * h + p, 1.0, ones_k)
    sel_q = sel_q.astype(BF16)
    sel_k = sel_k.astype(BF16)

    def expand(c):
        pieces = jnp.concatenate(_split_bf16(c, T), axis=1)
        return ((_mm(pieces, sel_q) + ones_q).astype(BF16), (_mm(pieces, sel_k) + ones_k).astype(BF16))

    chunk = ROW_TILE_REAL

    def ebody(t, _):
        r0 = pl.multiple_of(t * chunk, chunk)
        q, k = expand(c_ref[pl.ds(r0, chunk), :])
        qc_ref[pl.ds(r0, chunk), :] = q
        kc_ref[pl.ds(r0, chunk), :] = k
        return 0
    lax.fori_loop(0, n_real // chunk, ebody, 0)
    _, km = expand(c_ref[n_real:n_real + N_META, :])
    kc_ref[n_real:n_real + N_META, :] = km


def _decay(lf, *, n_batch, seq):
    n_real = n_batch * seq
    return pl.pallas_call(
        functools.partial(_decay_kernel, n_batch=n_batch, seq=seq),
        out_shape=[jax.ShapeDtypeStruct((n_real, LANES), BF16),
                   jax.ShapeDtypeStruct((n_real + N_META, LANES), BF16)],
        scratch_shapes=[pltpu.VMEM((n_real + N_META, LANES), F32)],
        compiler_params=pltpu.CompilerParams(vmem_limit_bytes=VMEM_LIMIT),
        name="decay",
    )(lf)


def _eye(n):
    return jnp.where(lax.broadcasted_iota(jnp.int32, (n, n), 0) == lax.broadcasted_iota(jnp.int32, (n, n), 1),
                     1.0, 0.0).astype(BF16)


def _stage_keys_values(kpad_ref, vt_ref, meta_k_refs, v_ref, mv_ref, *, seq):
    eye = _eye(HEAD_DIM)
    kpad_ref[...] = jnp.zeros_like(kpad_ref)
    for n, ref in enumerate(meta_k_refs):
        kpad_ref[0:N_META, n * HEAD_DIM:(n + 1) * HEAD_DIM] = ref[...]
    vpad = jnp.concatenate([mv_ref[...], jnp.zeros((LANES - N_META, HEAD_DIM), BF16)], axis=0)
    vt_ref[:, seq:seq + LANES] = _mm_nt(eye, vpad).astype(BF16)
    chunk = 2 * ATT_SUB
    for c0 in range(0, seq, chunk):
        vt_ref[:, c0:c0 + chunk] = _mm_nt(eye, v_ref[c0:c0 + chunk, :]).astype(BF16)


def _store_transposed(o_ref, r0, acc_t):
    o_ref[pl.ds(r0, ATT_SUB), :] = acc_t.T.astype(o_ref.dtype)


def _fox_head(q_ref, qc_ref, k_ref, kc_ref, v_ref, mk_ref, mkc_ref, mv_ref, o_ref, kpad_ref, vt_ref,
              sa_ref, sb_ref, *, seq):
    T = DECAY_TERMS
    h = pl.program_id(1)
    _stage_keys_values(kpad_ref, vt_ref, (mk_ref, mkc_ref), v_ref, mv_ref, seq=seq)

    lane = lax.broadcasted_iota(jnp.int32, (1, LANES), 1)
    head_lanes = (lane >= 2 * T * h) & (lane < 2 * T * (h + 1))
    meta_valid = lax.broadcasted_iota(jnp.int32, (LANES, ATT_SUB), 0) < N_META
    causal = (lax.broadcasted_iota(jnp.int32, (ATT_SUB, ATT_SUB), 0)
              <= lax.broadcasted_iota(jnp.int32, (ATT_SUB, ATT_SUB), 1))
    all_subs = tuple(range(ATT_NSUB))

    def keys(c0):
        return jnp.concatenate([k_ref[pl.ds(c0, ATT_SUB), :], kc_ref[pl.ds(c0, ATT_SUB), :]], axis=1)

    def online(s, vt, m, l, acc):
        m_new = jnp.maximum(m, jnp.max(s, axis=0, keepdims=True))
        alpha = jnp.exp2(m - m_new)
        p = jnp.exp2(s - m_new)
        l = alpha * l + jnp.sum(p, axis=0, keepdims=True)
        acc = alpha * acc + _mm(vt, p.astype(BF16))
        return m_new, l, acc

    def block(i):
        r0 = pl.multiple_of(i * ATT_TQ, ATT_TQ)
        qx = jnp.where(head_lanes, qc_ref[pl.ds(r0, ATT_TQ), :], jnp.zeros((), BF16))
        qa = jnp.concatenate([q_ref[pl.ds(r0, ATT_TQ), :], qx], axis=1)
        subs = [qa[n * ATT_SUB:(n + 1) * ATT_SUB] for n in all_subs]

        def issue(buf_ref, c0, which=all_subs):
            ka = keys(c0)
            for n in which:
                buf_ref[n] = _mm_nt(ka, subs[n])

        def consume(buf_ref, c0, state, masked=None, which=all_subs):
            vt = vt_ref[:, pl.ds(c0, ATT_SUB)]
            out = list(state)
            for n in which:
                s = buf_ref[n]
                if n == masked:
                    s = jnp.where(causal, s, -jnp.inf)
                out[n] = online(s, vt, *out[n])
            return tuple(out)

        issue(sa_ref, 0)

        state = []
        for qs in subs:
            s = jnp.where(meta_valid, _mm_nt(kpad_ref[...], qs), -jnp.inf)
            m = jnp.max(s, axis=0, keepdims=True)
            p = jnp.exp2(s - m)
            state.append((m, jnp.sum(p, axis=0, keepdims=True), _mm(vt_ref[:, seq:seq + LANES], p.astype(BF16))))

        def kv_pair(t, state):
            c0 = pl.multiple_of(t * (2 * ATT_SUB), 2 * ATT_SUB)
            c1 = pl.multiple_of(c0 + ATT_SUB, ATT_SUB)
            issue(sb_ref, c1)
            state = consume(sa_ref, c0, state)
            issue(sa_ref, pl.multiple_of(c0 + 2 * ATT_SUB, 2 * ATT_SUB))
            return consume(sb_ref, c1, state)

        def finish(state):
            bufs = (sa_ref, sb_ref)
            for kb in all_subs:
                if kb + 1 < ATT_NSUB:
                    issue(bufs[(kb + 1) % 2], pl.multiple_of(r0 + (kb + 1) * ATT_SUB, ATT_SUB), all_subs[kb + 1:])
                state = consume(bufs[kb % 2], pl.multiple_of(r0 + kb * ATT_SUB, ATT_SUB), state, kb, all_subs[kb:])
            for n, (m, l, acc) in enumerate(state):
                _store_transposed(o_ref, pl.multiple_of(r0 + n * ATT_SUB, ATT_SUB), acc / l)

        return tuple(state), kv_pair, finish

    return block


def _sb_head(q_ref, k_ref, v_ref, mk_ref, mv_ref, o_ref, kpad_ref, vt_ref, z_ref, *, seq):
    _stage_keys_values(kpad_ref, vt_ref, (mk_ref,), v_ref, mv_ref, seq=seq)

    def later_keys(n):
        return jnp.where(lax.broadcasted_iota(jnp.int32, (n, n), 0) < lax.broadcasted_iota(jnp.int32, (n, n), 1),
                         1.0, 0.0).astype(BF16)

    u_blk = later_keys(ATT_SUB)
    u_meta = later_keys(LANES)
    meta_valid = lax.broadcasted_iota(jnp.int32, (LANES, ATT_SUB), 0) < N_META
    strict = (lax.broadcasted_iota(jnp.int32, (ATT_SUB, ATT_SUB), 0)
              < lax.broadcasted_iota(jnp.int32, (ATT_SUB, ATT_SUB), 1))
    all_subs = tuple(range(ATT_NSUB))

    def steps(items, states, zs=None):
        states = list(states)
        if zs is None:
            zs = [_mm_nt(kblk, qs) for _, qs, kblk, _, _, _ in items]
        own, cums = [], []
        for z, (_, _, _, _, u, mask) in zip(zs, items):
            sp = jnp.maximum(jnp.log2(1.0 + jnp.exp2(jnp.minimum(z, MAX_EXP2))), z)
            if mask is not None:
                sp = jnp.where(mask, sp, 0.0)
            pieces = _split_bf16(sp, SUFFIX_TERMS)
            own.append((z - sp, sp[0:1, :]))
            cums.append(functools.reduce(lambda a, b: a + b, [_mm(u, p) for p in pieces]))
        for (log_beta, sp_first), cum, (n, _, _, vt, _, mask) in zip(own, cums, items):
            r, acc = states[n]
            w = jnp.exp2(log_beta - cum - r)
            if mask is not None:
                w = jnp.where(mask, w, 0.0)
            states[n] = (r + (cum[0:1, :] + sp_first), acc + _mm(vt, w.astype(BF16)))
        return tuple(states)

    def block(i):
        r0 = pl.multiple_of(i * ATT_TQ, ATT_TQ)
        q = [q_ref[pl.ds(pl.multiple_of(r0 + n * ATT_SUB, ATT_SUB), ATT_SUB), :] for n in all_subs]
        zero = (jnp.zeros((1, ATT_SUB), F32), jnp.zeros((HEAD_DIM, ATT_SUB), F32))

        def item(n, c0, mask):
            return (n, q[n], k_ref[pl.ds(c0, ATT_SUB), :], vt_ref[:, pl.ds(c0, ATT_SUB)], u_blk, mask)

        def pair_items(jj):
            items = []
            for d in (1, 2):
                c0 = pl.multiple_of(jnp.maximum(r0 - (2 * jj + d) * ATT_SUB, 0), ATT_SUB)
                items += [item(n, c0, None) for n in all_subs]
            return items

        def scores(items):
            return [_mm_nt(kblk, qs) for _, qs, kblk, _, _, _ in items]

        for n, z in enumerate(scores(pair_items(0))):
            z_ref[n] = z

        diag = []
        for kb in reversed(all_subs):
            c0 = pl.multiple_of(r0 + kb * ATT_SUB, ATT_SUB)
            diag += [item(n, c0, strict if n == kb else None) for n in all_subs[kb:]]
        states = steps(diag, (zero,) * ATT_NSUB)

        def kv_pair(jj, states):
            zs = [z_ref[n] for n in range(2 * ATT_NSUB)]
            ahead = scores(pair_items(jj + 1))
            states = steps(pair_items(jj), states, zs)
            for n, z in enumerate(ahead):
                z_ref[n] = z
            return states

        def finish(states):
            meta = [(n, q[n], kpad_ref[...], vt_ref[:, seq:seq + LANES], u_meta, meta_valid) for n in all_subs]
            for n, (_, acc) in enumerate(steps(meta, states)):
                _store_transposed(o_ref, pl.multiple_of(r0 + n * ATT_SUB, ATT_SUB), acc)

        return states, kv_pair, finish

    return block


def _attention_kernel(fq, fqc, fk, fkc, fv, fmk, fmkc, fmv, sq, sk, sv, smk, smv, of_ref, os_ref,
                      f_kpad, f_vt, f_sa, f_sb, s_kpad, s_vt, s_z, *, seq):
    fox = _fox_head(fq, fqc, fk, fkc, fv, fmk, fmkc, fmv, of_ref, f_kpad, f_vt, f_sa, f_sb, seq=seq)
    stick = _sb_head(sq, sk, sv, smk, smv, os_ref, s_kpad, s_vt, s_z, seq=seq)

    def q_block(i, _):
        f_state, f_pair, f_finish = fox(i)
        s_state, s_pair, s_finish = stick(i)

        def kv_pair(t, carry):
            s_next = s_pair(t, carry[1])
            return f_pair(t, carry[0]), s_next

        f_state, s_state = lax.fori_loop(0, i * (ATT_NSUB // 2), kv_pair, (f_state, s_state))
        f_finish(f_state)
        s_finish(s_state)
        return 0

    lax.fori_loop(0, seq // ATT_TQ, q_block, 0)


def _head_spec(rows, col_block0):
    return pl.BlockSpec((rows, HEAD_DIM), lambda b, h: (b, col_block0 + h))


def _meta_spec(n_real, col_block0):
    return pl.BlockSpec((N_META, HEAD_DIM), lambda b, h: (n_real // N_META, col_block0 + h))


def _attention(fox_qkv, qc, kc, sb_qkv, *, n_batch, seq):
    n_real = n_batch * seq
    shared = pl.BlockSpec((seq, LANES), lambda b, h: (b, 0))
    qkv_specs = [_head_spec(seq, 0), _head_spec(seq, N_HEADS), _head_spec(seq, 2 * N_HEADS)]
    meta_specs = [_meta_spec(n_real, N_HEADS), _meta_spec(n_real, 2 * N_HEADS)]
    score_buf = pltpu.VMEM((ATT_NSUB, ATT_SUB, ATT_SUB), F32)
    return pl.pallas_call(
        functools.partial(_attention_kernel, seq=seq),
        grid=(n_batch, N_HEADS),
        in_specs=[qkv_specs[0], shared, qkv_specs[1], shared, qkv_specs[2],
                  meta_specs[0], pl.BlockSpec((N_META, LANES), lambda b, h: (n_real // N_META, 0)), meta_specs[1],
                  *qkv_specs, *meta_specs],
        out_specs=[_head_spec(seq, 0), _head_spec(seq, 0)],
        out_shape=[jax.ShapeDtypeStruct((n_real, WIDTH), BF16)] * 2,
        scratch_shapes=[pltpu.VMEM((LANES, 2 * HEAD_DIM), BF16), pltpu.VMEM((HEAD_DIM, seq + LANES), BF16),
                        score_buf, score_buf,
                        pltpu.VMEM((LANES, HEAD_DIM), BF16), pltpu.VMEM((HEAD_DIM, seq + LANES), BF16),
                        pltpu.VMEM((2 * ATT_NSUB, ATT_SUB, ATT_SUB), F32)],
        compiler_params=_params(2),
        name="attention",
    )(fox_qkv, qc, fox_qkv, kc, fox_qkv, fox_qkv, kc, fox_qkv, sb_qkv, sb_qkv, sb_qkv, sb_qkv, sb_qkv)


def _mix_merge_kernel(of_ref, os_ref, gf_ref, gs_ref, wbf_ref, wbs_ref, o_ref, wbf_b, wbs_b):
    @pl.when(pl.program_id(1) == 0)
    def _():
        wbf_b[...] = wbf_ref[...].astype(BF16)
        wbs_b[...] = wbs_ref[...].astype(BF16)

    for rows in _row_chunks(of_ref.shape[0]):
        tf = _mm(of_ref[rows, :], wbf_b[...])
        ts = _mm(os_ref[rows, :], wbs_b[...])
        merged = gf_ref[rows, :].astype(F32) * tf + gs_ref[rows, :].astype(F32) * ts
        o_ref[rows, :] = merged.astype(o_ref.dtype)


def _mix_merge(o_fox, o_sb, gates, wbf, wbs, *, rows, tm, tn):
    ni, nj = rows // tm, D_MODEL // tn
    return pl.pallas_call(
        _mix_merge_kernel,
        grid=(nj, ni),
        in_specs=[
            pl.BlockSpec((tm, WIDTH), lambda j, i: (i, 0)),
            pl.BlockSpec((tm, WIDTH), lambda j, i: (i, 0)),
            pl.BlockSpec((tm, tn), lambda j, i: (i, j)),
            pl.BlockSpec((tm, tn), lambda j, i: (i, nj + j)),
            pl.BlockSpec((WIDTH, tn), lambda j, i: (0, j)),
            pl.BlockSpec((WIDTH, tn), lambda j, i: (0, j)),
        ],
        out_specs=pl.BlockSpec((tm, tn), lambda j, i: (i, j)),
        out_shape=jax.ShapeDtypeStruct((rows, D_MODEL), BF16),
        scratch_shapes=[pltpu.VMEM((WIDTH, tn), BF16), pltpu.VMEM((WIDTH, tn), BF16)],
        compiler_params=_params(2),
        name="mix_merge",
    )(o_fox, o_sb, gates, gates, wbf, wbs)


def _mix_proj_kernel(m_ref, w_ref, h_ref, o_ref, wb_ref):
    @pl.when(pl.program_id(1) == 0)
    def _():
        wb_ref[...] = w_ref[...].astype(BF16)

    for rows in _row_chunks(m_ref.shape[0]):
        o_ref[rows, :] = h_ref[rows, :] + _mm(m_ref[rows, :], wb_ref[...])


def _mix_proj(merged, wo, h, *, rows, tm, tn):
    ni, nj = rows // tm, D_MODEL // tn
    return pl.pallas_call(
        _mix_proj_kernel,
        grid=(nj, ni),
        in_specs=[
            pl.BlockSpec((tm, D_MODEL), lambda j, i: (i, 0)),
            pl.BlockSpec((D_MODEL, tn), lambda j, i: (0, j)),
            pl.BlockSpec((tm, tn), lambda j, i: (i, j)),
        ],
        out_specs=pl.BlockSpec((tm, tn), lambda j, i: (i, j)),
        out_shape=jax.ShapeDtypeStruct((rows, D_MODEL), F32),
        scratch_shapes=[pltpu.VMEM((D_MODEL, tn), BF16)],
        compiler_params=_params(2),
        name="mix_proj",
    )(merged, wo, h)


def kernel(x, meta_tokens, ffn1_norm, ffn1_w_gate, ffn1_w_up, ffn1_w_down, mix_norm, w_in, b_forget, fox_q_norm, fox_k_norm, w_branch_fox, w_branch_sb, w_out, ffn2_norm, ffn2_w_gate, ffn2_w_up, ffn2_w_down):
    n_batch, seq, _ = x.shape
    n_real = n_batch * seq
    n_all = n_real + N_META
    assert ffn1_norm.shape[0] == 1
    assert n_all % ROW_TILE_ALL == 0 and n_real % ROW_TILE_REAL == 0 and seq % ATT_TQ == 0

    wt = w_in[0].T
    r_f = 3 * WIDTH
    r_sb = r_f + N_HEADS
    r_gate = r_sb + 3 * WIDTH
    w_f = jnp.pad(w_in[0][:, r_f:r_sb], ((0, 0), (0, LANES - N_HEADS)))
    b_f = jnp.pad(b_forget[0].reshape(1, N_HEADS), ((0, 0), (0, LANES - N_HEADS)))
    fox_gain = jnp.concatenate([fox_q_norm[0].reshape(1, WIDTH) * (SCALE * LOG2E), fox_k_norm[0].reshape(1, WIDTH),
                                jnp.ones((1, WIDTH), F32)], axis=1)
    sb_scale = jnp.concatenate([jnp.full((1, WIDTH), SCALE * LOG2E, F32), jnp.ones((1, 2 * WIDTH), F32)], axis=1)

    h1, n1, log_f = _ffn(x.reshape(n_real, D_MODEL), meta_tokens.astype(F32), ffn1_norm[0], ffn1_w_gate[0],
                         ffn1_w_up[0], ffn1_w_down[0], rows=n_all, tm=ROW_TILE_ALL, next_gain=mix_norm[0],
                         forget=(w_f, b_f))

    proj = functools.partial(_proj, n1, rows=n_all, tm=ROW_TILE_ALL)
    fox_qkv = proj(wt, 0, 3 * WIDTH, fox_gain, tn=PROJ_TILE_N, epilogue=_fox_epilogue,
                   out_dtype=BF16, name="proj_fox")
    sb_qkv = proj(wt, r_sb, 3 * WIDTH, sb_scale, tn=PROJ_TILE_N, epilogue=_scale_epilogue,
                  out_dtype=BF16, name="proj_sb")
    gates = _proj(n1, wt, r_gate, 2 * D_MODEL, None, rows=n_real, tm=ROW_TILE_REAL,
                  tn=PROJ_TILE_N, epilogue=_gate_epilogue, out_dtype=BF16, name="proj_gates")

    qc, kc = _decay(log_f, n_batch=n_batch, seq=seq)
    o_fox, o_sb = _attention(fox_qkv, qc, kc, sb_qkv, n_batch=n_batch, seq=seq)

    merged = _mix_merge(o_fox, o_sb, gates, w_branch_fox[0], w_branch_sb[0],
                        rows=n_real, tm=ROW_TILE_REAL, tn=PROJ_TILE_N)
    h2 = _mix_proj(merged, w_out[0], h1, rows=n_real, tm=ROW_TILE_REAL, tn=PROJ_TILE_N)
    h3 = _ffn(h2, None, ffn2_norm[0], ffn2_w_gate[0], ffn2_w_up[0], ffn2_w_down[0],
              rows=n_real, tm=ROW_TILE_REAL)
    return h3.reshape(n_batch, seq, D_MODEL)
```

```python
import functools
import math

import jax
import jax.numpy as jnp
from jax import lax
from jax.experimental import pallas as pl
from jax.experimental.pallas import tpu as pltpu

F32 = jnp.float32
BF16 = jnp.bfloat16

D_MODEL = 2048
D_FF = 5632
N_META = 16
HEAD_DIM = 128
N_HEADS = 8
WIDTH = N_HEADS * HEAD_DIM
RMS_EPS = 1e-6
FFN_RESIDUAL_WEIGHT = 0.5
SCALE = HEAD_DIM ** -0.5
LOG2E = math.log2(math.e)

LANES = 128
SUBLANES = 8
BF16_ROWS = 16
VMEM_LIMIT = 58 * 2**20

ROW_TILE_ALL = 912
ROW_TILE_REAL = 1024
FFN_TILE_F = 256
PROJ_TILE_N = 1024
ATT_SUB = 256
ATT_NSUB = 4
ATT_TQ = ATT_SUB * ATT_NSUB
DECAY_TERMS = 3
SUFFIX_TERMS = 1
MAX_EXP2 = 126.0


def _mm(a, b):
    return jnp.dot(a, b, preferred_element_type=F32)


def _mm_nt(a, b):
    return lax.dot_general(a, b, (((1,), (1,)), ((), ())), preferred_element_type=F32)


def _split_bf16(x, n):
    parts = [x.astype(BF16)]
    for _ in range(n - 1):
        x = x - parts[-1].astype(F32)
        parts.append(x.astype(BF16))
    return parts


def _rmsnorm_rows(h, gain):
    ms = jnp.mean(h * h, axis=-1, keepdims=True)
    return h * lax.rsqrt(ms + RMS_EPS) * gain


def _log_sigmoid(z):
    return jnp.minimum(z, 0.0) - jnp.log(1.0 + jnp.exp(-jnp.abs(z)))


def _params(n_grid_axes):
    return pltpu.CompilerParams(dimension_semantics=("arbitrary",) * n_grid_axes,
                                vmem_limit_bytes=VMEM_LIMIT)


def _ffn_kernel(*refs, meta_row, emit_norm):
    refs = list(refs)
    x_ref = refs.pop(0)
    meta_ref = refs.pop(0) if meta_row is not None else None
    gain_ref, wg_ref, wu_ref, wd_ref = refs[:4]
    refs = refs[4:]
    if emit_norm:
        ngain_ref, wf_ref, bf_ref, o_ref, n_ref, lf_ref, xn_ref = refs
    else:
        o_ref, xn_ref = refs
    i = pl.program_id(0)
    j = pl.program_id(1)

    @pl.when(j == 0)
    def _():
        if meta_row is None:
            o_ref[...] = x_ref[...]
        else:
            last = pl.num_programs(0) - 1

            @pl.when(i < last)
            def _():
                o_ref[...] = x_ref[...]

            @pl.when(i == last)
            def _():
                o_ref[0:meta_row, :] = x_ref[0:meta_row, :]
                o_ref[meta_row:meta_row + N_META, :] = meta_ref[...]
        xn_ref[...] = _rmsnorm_rows(o_ref[...], gain_ref[...]).astype(BF16)

    xn = xn_ref[...]
    g = _mm(xn, wg_ref[...].astype(BF16))
    u = _mm(xn, wu_ref[...].astype(BF16))
    a = (g * jax.nn.sigmoid(g)) * (u * FFN_RESIDUAL_WEIGHT)
    o_ref[...] += _mm(a.astype(BF16), wd_ref[...].astype(BF16))

    if emit_norm:
        @pl.when(j == pl.num_programs(1) - 1)
        def _():
            n = _rmsnorm_rows(o_ref[...], ngain_ref[...]).astype(BF16)
            n_ref[...] = n
            lf_ref[...] = _log_sigmoid(_mm(n, wf_ref[...].astype(BF16)) + bf_ref[...]) * LOG2E


def _ffn(x, meta, gain, wg, wu, wd, *, rows, tm, next_gain=None, forget=None):
    ni = rows // tm
    nj = D_FF // FFN_TILE_F
    emit_norm = next_gain is not None
    in_specs = [pl.BlockSpec((tm, D_MODEL), lambda i, j: (i, 0))]
    args = [x]
    meta_row = None
    if meta is not None:
        meta_row = x.shape[0] - (ni - 1) * tm
        assert meta_row + N_META == tm
        in_specs.append(pl.BlockSpec((N_META, D_MODEL), lambda i, j: (0, 0)))
        args.append(meta)
    in_specs += [
        pl.BlockSpec((1, D_MODEL), lambda i, j: (0, 0)),
        pl.BlockSpec((D_MODEL, FFN_TILE_F), lambda i, j: (0, j)),
        pl.BlockSpec((D_MODEL, FFN_TILE_F), lambda i, j: (0, j)),
        pl.BlockSpec((FFN_TILE_F, D_MODEL), lambda i, j: (j, 0)),
    ]
    args += [gain.reshape(1, D_MODEL), wg, wu, wd]
    out_shape = [jax.ShapeDtypeStruct((rows, D_MODEL), F32)]
    out_specs = [pl.BlockSpec((tm, D_MODEL), lambda i, j: (i, 0))]
    if emit_norm:
        w_f, b_f = forget
        in_specs += [pl.BlockSpec((1, D_MODEL), lambda i, j: (0, 0)),
                     pl.BlockSpec((D_MODEL, LANES), lambda i, j: (0, 0)),
                     pl.BlockSpec((1, LANES), lambda i, j: (0, 0))]
        args += [next_gain.reshape(1, D_MODEL), w_f, b_f]
        out_shape += [jax.ShapeDtypeStruct((rows, D_MODEL), BF16), jax.ShapeDtypeStruct((rows, LANES), F32)]
        out_specs += [pl.BlockSpec((tm, D_MODEL), lambda i, j: (i, 0)), pl.BlockSpec((tm, LANES), lambda i, j: (i, 0))]
    res = pl.pallas_call(
        functools.partial(_ffn_kernel, meta_row=meta_row, emit_norm=emit_norm),
        grid=(ni, nj),
        in_specs=in_specs,
        out_specs=out_specs,
        out_shape=out_shape,
        scratch_shapes=[pltpu.VMEM((tm, D_MODEL), BF16)],
        compiler_params=_params(2),
        name="ffn_norm" if emit_norm else "ffn",
    )(*args)
    return res if emit_norm else res[0]


def _proj_kernel(n_ref, w_ref, *refs, epilogue):
    p_ref = refs[0] if len(refs) == 3 else None
    o_ref, wb_ref = refs[-2:]

    @pl.when(pl.program_id(1) == 0)
    def _():
        wb_ref[...] = w_ref[...].T.astype(BF16)

    def run(fn):
        for rows in _row_chunks(n_ref.shape[0]):
            o_ref[rows, :] = fn(_mm(n_ref[rows, :], wb_ref[...])).astype(o_ref.dtype)

    epilogue(run, p_ref, pl.program_id(0))


def _row_chunks(tm):
    n = next(n for n in (4, 3, 2, 1) if tm % (n * BF16_ROWS) == 0)
    return [slice(c * (tm // n), (c + 1) * (tm // n)) for c in range(n)]


def _fox_epilogue(run, p_ref, j):
    def head_norm(y):
        heads = [slice(h * HEAD_DIM, (h + 1) * HEAD_DIM) for h in range(y.shape[1] // HEAD_DIM)]
        return jnp.concatenate([_rmsnorm_rows(y[:, sl], p_ref[:, sl]) for sl in heads], axis=1)

    pl.when(j < 2)(lambda: run(head_norm))
    pl.when(j >= 2)(lambda: run(lambda y: y))


def _scale_epilogue(run, p_ref, j):
    run(lambda y: y * p_ref[...])


def _gate_epilogue(run, p_ref, j):
    run(jax.nn.sigmoid)


def _proj(n, wt, row0, n_cols, p, *, rows, tm, tn, epilogue, out_dtype, name):
    assert n_cols % tn == 0
    nj, ni = n_cols // tn, rows // tm
    if row0 % tn == 0:
        w_spec = pl.BlockSpec((tn, D_MODEL), lambda j, i: (row0 // tn + j, 0))
    else:
        assert row0 % SUBLANES == 0 and tn % SUBLANES == 0
        w_spec = pl.BlockSpec((pl.Element(tn), pl.Element(D_MODEL)),
                              lambda j, i: ((row0 // SUBLANES + j * (tn // SUBLANES)) * SUBLANES, 0))
    in_specs = [pl.BlockSpec((tm, D_MODEL), lambda j, i: (i, 0)), w_spec]
    args = [n, wt]
    if p is not None:
        in_specs.append(pl.BlockSpec((1, tn), lambda j, i: (0, j)))
        args.append(p)
    return pl.pallas_call(
        functools.partial(_proj_kernel, epilogue=epilogue),
        grid=(nj, ni),
        in_specs=in_specs,
        out_specs=pl.BlockSpec((tm, tn), lambda j, i: (i, j)),
        out_shape=jax.ShapeDtypeStruct((rows, n_cols), out_dtype),
        scratch_shapes=[pltpu.VMEM((D_MODEL, tn), BF16)],
        compiler_params=_params(2),
        name=name,
    )(*args)


def _decay_kernel(lf_ref, qc_ref, kc_ref, c_ref, *, n_batch, seq):
    T = DECAY_TERMS
    n_real = n_batch * seq
    r_i = lax.broadcasted_iota(jnp.int32, (LANES, LANES), 0)
    c_i = lax.broadcasted_iota(jnp.int32, (LANES, LANES), 1)
    tri = jnp.where(r_i >= c_i, 1.0, 0.0).astype(BF16)

    def cum_block(x, carry):
        acc = carry
        for p in _split_bf16(x, T):
            acc = acc + _mm(tri, p)
        return acc

    xm = jnp.concatenate([lf_ref[n_real:n_real + N_META, :], jnp.zeros((LANES - N_META, LANES), F32)], axis=0)
    cm = cum_block(xm, jnp.zeros((1, LANES), F32))
    c_ref[n_real:n_real + N_META, :] = cm[:N_META, :]
    base = cm[N_META - 1:N_META, :]

    def body(t, carries):
        out = []
        for b, carry in enumerate(carries):
            r0 = pl.multiple_of(b * seq + t * LANES, LANES)
            c = cum_block(lf_ref[pl.ds(r0, LANES), :], carry)
            c_ref[pl.ds(r0, LANES), :] = c
            out.append(c[LANES - 1:LANES, :])
        return tuple(out)
    lax.fori_loop(0, seq // LANES, body, (base,) * n_batch)

    k_i = lax.broadcasted_iota(jnp.int32, (T * LANES, LANES), 0)
    n_i = lax.broadcasted_iota(jnp.int32, (T * LANES, LANES), 1)
    sel_q = jnp.zeros((T * LANES, LANES), F32)
    sel_k = jnp.zeros((T * LANES, LANES), F32)
    lane = lax.broadcasted_iota(jnp.int32, (1, LANES), 1)
    ones_q = jnp.zeros((1, LANES), F32)
    ones_k = jnp.zeros((1, LANES), F32)
    for h in range(N_HEADS):
        for p in range(T):
            row = p * LANES + h
            sel_q = jnp.where((k_i == row) & (n_i == 2 * T * h + p), 1.0, sel_q)
            sel_k = jnp.where((k_i == row) & (n_i == 2 * T * h + T + p), -1.0, sel_k)
            ones_q = jnp.where(lane == 2 * T * h + T + p, 1.0, ones_q)
            ones_k = jnp.where(lane == 2 * T * h + p, 1.0, ones_k)
    sel_q = sel_q.astype(BF16)
    sel_k = sel_k.astype(BF16)

    def expand(c):
        pieces = jnp.concatenate(_split_bf16(c, T), axis=1)
        return ((_mm(pieces, sel_q) + ones_q).astype(BF16), (_mm(pieces, sel_k) + ones_k).astype(BF16))

    chunk = ROW_TILE_REAL

    def ebody(t, _):
        r0 = pl.multiple_of(t * chunk, chunk)
        q, k = expand(c_ref[pl.ds(r0, chunk), :])
        qc_ref[pl.ds(r0, chunk), :] = q
        kc_ref[pl.ds(r0, chunk), :] = k
        return 0
    lax.fori_loop(0, n_real // chunk, ebody, 0)
    _, km = expand(c_ref[n_real:n_real + N_META, :])
    kc_ref[n_real:n_real + N_META, :] = km


def _decay(lf, *, n_batch, seq):
    n_real = n_batch * seq
    return pl.pallas_call(
        functools.partial(_decay_kernel, n_batch=n_batch, seq=seq),
        out_shape=[jax.ShapeDtypeStruct((n_real, LANES), BF16),
                   jax.ShapeDtypeStruct((n_real + N_META, LANES), BF16)],
        scratch_shapes=[pltpu.VMEM((n_real + N_META, LANES), F32)],
        compiler_params=pltpu.CompilerParams(vmem_limit_bytes=VMEM_LIMIT),
        name="decay",
    )(lf)


def _eye(n):
    return jnp.where(lax.broadcasted_iota(jnp.int32, (n, n), 0) == lax.broadcasted_iota(jnp.int32, (n, n), 1),
                     1.0, 0.0).astype(BF16)


def _stage_keys_values(kpad_ref, vt_ref, meta_k_refs, v_ref, mv_ref, *, seq):
    eye = _eye(HEAD_DIM)
    kpad_ref[...] = jnp.zeros_like(kpad_ref)
    for n, ref in enumerate(meta_k_refs):
        kpad_ref[0:N_META, n * HEAD_DIM:(n + 1) * HEAD_DIM] = ref[...]
    vpad = jnp.concatenate([mv_ref[...], jnp.zeros((LANES - N_META, HEAD_DIM), BF16)], axis=0)
    vt_ref[:, seq:seq + LANES] = _mm_nt(eye, vpad).astype(BF16)
    chunk = 2 * ATT_SUB
    for c0 in range(0, seq, chunk):
        vt_ref[:, c0:c0 + chunk] = _mm_nt(eye, v_ref[c0:c0 + chunk, :]).astype(BF16)


def _store_transposed(o_ref, r0, acc_t):
    o_ref[pl.ds(r0, ATT_SUB), :] = acc_t.T.astype(o_ref.dtype)


def _fox_head(q_ref, qc_ref, k_ref, kc_ref, v_ref, mk_ref, mkc_ref, mv_ref, o_ref, kpad_ref, vt_ref,
              sa_ref, sb_ref, *, seq):
    T = DECAY_TERMS
    h = pl.program_id(1)
    _stage_keys_values(kpad_ref, vt_ref, (mk_ref, mkc_ref), v_ref, mv_ref, seq=seq)

    lane = lax.broadcasted_iota(jnp.int32, (1, LANES), 1)
    head_lanes = (lane >= 2 * T * h) & (lane < 2 * T * (h + 1))
    meta_valid = lax.broadcasted_iota(jnp.int32, (LANES, ATT_SUB), 0) < N_META
    causal = (lax.broadcasted_iota(jnp.int32, (ATT_SUB, ATT_SUB), 0)
              <= lax.broadcasted_iota(jnp.int32, (ATT_SUB, ATT_SUB), 1))
    all_subs = tuple(range(ATT_NSUB))

    def keys(c0):
        return jnp.concatenate([k_ref[pl.ds(c0, ATT_SUB), :], kc_ref[pl.ds(c0, ATT_SUB), :]], axis=1)

    def online(s, vt, m, l, acc):
        m_new = jnp.maximum(m, jnp.max(s, axis=0, keepdims=True))
        alpha = jnp.exp2(m - m_new)
        p = jnp.exp2(s - m_new)
        l = alpha * l + jnp.sum(p, axis=0, keepdims=True)
        acc = alpha * acc + _mm(vt, p.astype(BF16))
        return m_new, l, acc

    def block(i):
        r0 = pl.multiple_of(i * ATT_TQ, ATT_TQ)
        qx = jnp.where(head_lanes, qc_ref[pl.ds(r0, ATT_TQ), :], jnp.zeros((), BF16))
        qa = jnp.concatenate([q_ref[pl.ds(r0, ATT_TQ), :], qx], axis=1)
        subs = [qa[n * ATT_SUB:(n + 1) * ATT_SUB] for n in all_subs]

        def issue(buf_ref, c0, which=all_subs):
            ka = keys(c0)
            for n in which:
                buf_ref[n] = _mm_nt(ka, subs[n])

        def consume(buf_ref, c0, state, masked=None, which=all_subs):
            vt = vt_ref[:, pl.ds(c0, ATT_SUB)]
            out = list(state)
            for n in which:
                s = buf_ref[n]
                if n == masked:
                    s = jnp.where(causal, s, -jnp.inf)
                out[n] = online(s, vt, *out[n])
            return tuple(out)

        issue(sa_ref, 0)

        state = []
        for qs in subs:
            s = jnp.where(meta_valid, _mm_nt(kpad_ref[...], qs), -jnp.inf)
            m = jnp.max(s, axis=0, keepdims=True)
            p = jnp.exp2(s - m)
            state.append((m, jnp.sum(p, axis=0, keepdims=True), _mm(vt_ref[:, seq:seq + LANES], p.astype(BF16))))

        def kv_pair(t, state):
            c0 = pl.multiple_of(t * (2 * ATT_SUB), 2 * ATT_SUB)
            c1 = pl.multiple_of(c0 + ATT_SUB, ATT_SUB)
            issue(sb_ref, c1)
            state = consume(sa_ref, c0, state)
            issue(sa_ref, pl.multiple_of(c0 + 2 * ATT_SUB, 2 * ATT_SUB))
            return consume(sb_ref, c1, state)

        def finish(state):
            bufs = (sa_ref, sb_ref)
            for kb in all_subs:
                if kb + 1 < ATT_NSUB:
                    issue(bufs[(kb + 1) % 2], pl.multiple_of(r0 + (kb + 1) * ATT_SUB, ATT_SUB), all_subs[kb + 1:])
                state = consume(bufs[kb % 2], pl.multiple_of(r0 + kb * ATT_SUB, ATT_SUB), state, kb, all_subs[kb:])
            for n, (m, l, acc) in enumerate(state):
                _store_transposed(o_ref, pl.multiple_of(r0 + n * ATT_SUB, ATT_SUB), acc / l)

        return tuple(state), kv_pair, finish

    return block


def _sb_head(q_ref, k_ref, v_ref, mk_ref, mv_ref, o_ref, kpad_ref, vt_ref, *, seq):
    _stage_keys_values(kpad_ref, vt_ref, (mk_ref,), v_ref, mv_ref, seq=seq)

    def later_keys(n):
        return jnp.where(lax.broadcasted_iota(jnp.int32, (n, n), 0) < lax.broadcasted_iota(jnp.int32, (n, n), 1),
                         1.0, 0.0).astype(BF16)

    u_blk = later_keys(ATT_SUB)
    u_meta = later_keys(LANES)
    meta_valid = lax.broadcasted_iota(jnp.int32, (LANES, ATT_SUB), 0) < N_META
    strict = (lax.broadcasted_iota(jnp.int32, (ATT_SUB, ATT_SUB), 0)
              < lax.broadcasted_iota(jnp.int32, (ATT_SUB, ATT_SUB), 1))
    all_subs = tuple(range(ATT_NSUB))

    def steps(items, states):
        states = list(states)
        zs = [_mm_nt(kblk, qs) for _, qs, kblk, _, _, _ in items]
        own, cums = [], []
        for z, (_, _, _, _, u, mask) in zip(zs, items):
            sp = jnp.maximum(jnp.log2(1.0 + jnp.exp2(jnp.minimum(z, MAX_EXP2))), z)
            if mask is not None:
                sp = jnp.where(mask, sp, 0.0)
            pieces = _split_bf16(sp, SUFFIX_TERMS)
            own.append((z - sp, sp[0:1, :]))
            cums.append(functools.reduce(lambda a, b: a + b, [_mm(u, p) for p in pieces]))
        for (log_beta, sp_first), cum, (n, _, _, vt, _, mask) in zip(own, cums, items):
            r, acc = states[n]
            w = jnp.exp2(log_beta - cum - r)
            if mask is not None:
                w = jnp.where(mask, w, 0.0)
            states[n] = (r + (cum[0:1, :] + sp_first), acc + _mm(vt, w.astype(BF16)))
        return tuple(states)

    def block(i):
        r0 = pl.multiple_of(i * ATT_TQ, ATT_TQ)
        q = [q_ref[pl.ds(pl.multiple_of(r0 + n * ATT_SUB, ATT_SUB), ATT_SUB), :] for n in all_subs]
        zero = (jnp.zeros((1, ATT_SUB), F32), jnp.zeros((HEAD_DIM, ATT_SUB), F32))

        def item(n, c0, mask):
            return (n, q[n], k_ref[pl.ds(c0, ATT_SUB), :], vt_ref[:, pl.ds(c0, ATT_SUB)], u_blk, mask)

        def pair_items(jj):
            items = []
            for d in (1, 2):
                c0 = pl.multiple_of(r0 - (2 * jj + d) * ATT_SUB, ATT_SUB)
                items += [item(n, c0, None) for n in all_subs]
            return items

        diag = []
        for kb in reversed(all_subs):
            c0 = pl.multiple_of(r0 + kb * ATT_SUB, ATT_SUB)
            diag += [item(n, c0, strict if n == kb else None) for n in all_subs[kb:]]
        states = steps(diag, (zero,) * ATT_NSUB)

        def kv_pair(jj, states):
            return steps(pair_items(jj), states)

        def finish(states):
            meta = [(n, q[n], kpad_ref[...], vt_ref[:, seq:seq + LANES], u_meta, meta_valid) for n in all_subs]
            for n, (_, acc) in enumerate(steps(meta, states)):
                _store_transposed(o_ref, pl.multiple_of(r0 + n * ATT_SUB, ATT_SUB), acc)

        return states, kv_pair, finish

    return block


def _attention_kernel(fq, fqc, fk, fkc, fv, fmk, fmkc, fmv, sq, sk, sv, smk, smv, of_ref, os_ref,
                      f_kpad, f_vt, f_sa, f_sb, s_kpad, s_vt, *, seq):
    fox = _fox_head(fq, fqc, fk, fkc, fv, fmk, fmkc, fmv, of_ref, f_kpad, f_vt, f_sa, f_sb, seq=seq)
    stick = _sb_head(sq, sk, sv, smk, smv, os_ref, s_kpad, s_vt, seq=seq)

    def q_block(i, _):
        f_state, f_pair, f_finish = fox(i)
        s_state, s_pair, s_finish = stick(i)

        def kv_pair(t, carry):
            s_next = s_pair(t, carry[1])
            return f_pair(t, carry[0]), s_next

        f_state, s_state = lax.fori_loop(0, i * (ATT_NSUB // 2), kv_pair, (f_state, s_state))
        f_finish(f_state)
        s_finish(s_state)
        return 0

    lax.fori_loop(0, seq // ATT_TQ, q_block, 0)


def _head_spec(rows, col_block0):
    return pl.BlockSpec((rows, HEAD_DIM), lambda b, h: (b, col_block0 + h))


def _meta_spec(n_real, col_block0):
    return pl.BlockSpec((N_META, HEAD_DIM), lambda b, h: (n_real // N_META, col_block0 + h))


def _attention(fox_qkv, qc, kc, sb_qkv, *, n_batch, seq):
    n_real = n_batch * seq
    shared = pl.BlockSpec((seq, LANES), lambda b, h: (b, 0))
    qkv_specs = [_head_spec(seq, 0), _head_spec(seq, N_HEADS), _head_spec(seq, 2 * N_HEADS)]
    meta_specs = [_meta_spec(n_real, N_HEADS), _meta_spec(n_real, 2 * N_HEADS)]
    score_buf = pltpu.VMEM((ATT_NSUB, ATT_SUB, ATT_SUB), F32)
    return pl.pallas_call(
        functools.partial(_attention_kernel, seq=seq),
        grid=(n_batch, N_HEADS),
        in_specs=[qkv_specs[0], shared, qkv_specs[1], shared, qkv_specs[2],
                  meta_specs[0], pl.BlockSpec((N_META, LANES), lambda b, h: (n_real // N_META, 0)), meta_specs[1],
                  *qkv_specs, *meta_specs],
        out_specs=[_head_spec(seq, 0), _head_spec(seq, 0)],
        out_shape=[jax.ShapeDtypeStruct((n_real, WIDTH), BF16)] * 2,
        scratch_shapes=[pltpu.VMEM((LANES, 2 * HEAD_DIM), BF16), pltpu.VMEM((HEAD_DIM, seq + LANES), BF16),
                        score_buf, score_buf,
                        pltpu.VMEM((LANES, HEAD_DIM), BF16), pltpu.VMEM((HEAD_DIM, seq + LANES), BF16)],
        compiler_params=_params(2),
        name="attention",
    )(fox_qkv, qc, fox_qkv, kc, fox_qkv, fox_qkv, kc, fox_qkv, sb_qkv, sb_qkv, sb_qkv, sb_qkv, sb_qkv)


def _mix_merge_kernel(of_ref, os_ref, gf_ref, gs_ref, wbf_ref, wbs_ref, o_ref, wbf_b, wbs_b):
    @pl.when(pl.program_id(1) == 0)
    def _():
        wbf_b[...] = wbf_ref[...].astype(BF16)
        wbs_b[...] = wbs_ref[...].astype(BF16)

    for rows in _row_chunks(of_ref.shape[0]):
        tf = _mm(of_ref[rows, :], wbf_b[...])
        ts = _mm(os_ref[rows, :], wbs_b[...])
        merged = gf_ref[rows, :].astype(F32) * tf + gs_ref[rows, :].astype(F32) * ts
        o_ref[rows, :] = merged.astype(o_ref.dtype)


def _mix_merge(o_fox, o_sb, gates, wbf, wbs, *, rows, tm, tn):
    ni, nj = rows // tm, D_MODEL // tn
    return pl.pallas_call(
        _mix_merge_kernel,
        grid=(nj, ni),
        in_specs=[
            pl.BlockSpec((tm, WIDTH), lambda j, i: (i, 0)),
            pl.BlockSpec((tm, WIDTH), lambda j, i: (i, 0)),
            pl.BlockSpec((tm, tn), lambda j, i: (i, j)),
            pl.BlockSpec((tm, tn), lambda j, i: (i, nj + j)),
            pl.BlockSpec((WIDTH, tn), lambda j, i: (0, j)),
            pl.BlockSpec((WIDTH, tn), lambda j, i: (0, j)),
        ],
        out_specs=pl.BlockSpec((tm, tn), lambda j, i: (i, j)),
        out_shape=jax.ShapeDtypeStruct((rows, D_MODEL), BF16),
        scratch_shapes=[pltpu.VMEM((WIDTH, tn), BF16), pltpu.VMEM((WIDTH, tn), BF16)],
        compiler_params=_params(2),
        name="mix_merge",
    )(o_fox, o_sb, gates, gates, wbf, wbs)


def _mix_proj_kernel(m_ref, w_ref, h_ref, o_ref, wb_ref):
    @pl.when(pl.program_id(1) == 0)
    def _():
        wb_ref[...] = w_ref[...].astype(BF16)

    for rows in _row_chunks(m_ref.shape[0]):
        o_ref[rows, :] = h_ref[rows, :] + _mm(m_ref[rows, :], wb_ref[...])


def _mix_proj(merged, wo, h, *, rows, tm, tn):
    ni, nj = rows // tm, D_MODEL // tn
    return pl.pallas_call(
        _mix_proj_kernel,
        grid=(nj, ni),
        in_specs=[
            pl.BlockSpec((tm, D_MODEL), lambda j, i: (i, 0)),
            pl.BlockSpec((D_MODEL, tn), lambda j, i: (0, j)),
            pl.BlockSpec((tm, tn), lambda j, i: (i, j)),
        ],
        out_specs=pl.BlockSpec((tm, tn), lambda j, i: (i, j)),
        out_shape=jax.ShapeDtypeStruct((rows, D_MODEL), F32),
        scratch_shapes=[pltpu.VMEM((D_MODEL, tn), BF16)],
        compiler_params=_params(2),
        name="mix_proj",
    )(merged, wo, h)


def kernel(x, meta_tokens, ffn1_norm, ffn1_w_gate, ffn1_w_up, ffn1_w_down, mix_norm, w_in, b_forget, fox_q_norm, fox_k_norm, w_branch_fox, w_branch_sb, w_out, ffn2_norm, ffn2_w_gate, ffn2_w_up, ffn2_w_down):
    n_batch, seq, _ = x.shape
    n_real = n_batch * seq
    n_all = n_real + N_META
    assert ffn1_norm.shape[0] == 1
    assert n_all % ROW_TILE_ALL == 0 and n_real % ROW_TILE_REAL == 0 and seq % ATT_TQ == 0

    wt = w_in[0].T
    r_f = 3 * WIDTH
    r_sb = r_f + N_HEADS
    r_gate = r_sb + 3 * WIDTH
    w_f = jnp.pad(w_in[0][:, r_f:r_sb], ((0, 0), (0, LANES - N_HEADS)))
    b_f = jnp.pad(b_forget[0].reshape(1, N_HEADS), ((0, 0), (0, LANES - N_HEADS)))
    fox_gain = jnp.concatenate([fox_q_norm[0].reshape(1, WIDTH) * (SCALE * LOG2E), fox_k_norm[0].reshape(1, WIDTH),
                                jnp.ones((1, WIDTH), F32)], axis=1)
    sb_scale = jnp.concatenate([jnp.full((1, WIDTH), SCALE * LOG2E, F32), jnp.ones((1, 2 * WIDTH), F32)], axis=1)

    h1, n1, log_f = _ffn(x.reshape(n_real, D_MODEL), meta_tokens.astype(F32), ffn1_norm[0], ffn1_w_gate[0],
                         ffn1_w_up[0], ffn1_w_down[0], rows=n_all, tm=ROW_TILE_ALL, next_gain=mix_norm[0],
                         forget=(w_f, b_f))

    proj = functools.partial(_proj, n1, rows=n_all, tm=ROW_TILE_ALL)
    fox_qkv = proj(wt, 0, 3 * WIDTH, fox_gain, tn=PROJ_TILE_N, epilogue=_fox_epilogue,
                   out_dtype=BF16, name="proj_fox")
    sb_qkv = proj(wt, r_sb, 3 * WIDTH, sb_scale, tn=PROJ_TILE_N, epilogue=_scale_epilogue,
                  out_dtype=BF16, name="proj_sb")
    gates = _proj(n1, wt, r_gate, 2 * D_MODEL, None, rows=n_real, tm=ROW_TILE_REAL,
                  tn=PROJ_TILE_N, epilogue=_gate_epilogue, out_dtype=BF16, name="proj_gates")

    qc, kc = _decay(log_f, n_batch=n_batch, seq=seq)
    o_fox, o_sb = _attention(fox_qkv, qc, kc, sb_qkv, n_batch=n_batch, seq=seq)

    merged = _mix_merge(o_fox, o_sb, gates, w_branch_fox[0], w_branch_sb[0],
                        rows=n_real, tm=ROW_TILE_REAL, tn=PROJ_TILE_N)
    h2 = _mix_proj(merged, w_out[0], h1, rows=n_real, tm=ROW_TILE_REAL, tn=PROJ_TILE_N)
    h3 = _ffn(h2, None, ffn2_norm[0], ffn2_w_gate[0], ffn2_w_up[0], ffn2_w_down[0],
              rows=n_real, tm=ROW_TILE_REAL)
    return h3.reshape(n_batch, seq, D_MODEL)
```

```python
import functools
import math

import jax
import jax.numpy as jnp
from jax import lax
from jax.experimental import pallas as pl
from jax.experimental.pallas import tpu as pltpu

F32 = jnp.float32
BF16 = jnp.bfloat16

D_MODEL = 2048
D_FF = 5632
N_META = 16
HEAD_DIM = 128
N_HEADS = 8
WIDTH = N_HEADS * HEAD_DIM
RMS_EPS = 1e-6
FFN_RESIDUAL_WEIGHT = 0.5
SCALE = HEAD_DIM ** -0.5
LOG2E = math.log2(math.e)

LANES = 128
SUBLANES = 8
BF16_ROWS = 16
VMEM_LIMIT = 58 * 2**20

ROW_TILE_ALL = 912
ROW_TILE_REAL = 1024
FFN_TILE_F = 256
PROJ_TILE_N = 1024
ATT_SUB = 256
ATT_NSUB = 4
ATT_TQ = ATT_SUB * ATT_NSUB
DECAY_TERMS = 3
SUFFIX_TERMS = 1
MAX_EXP2 = 126.0


def _mm(a, b):
    return jnp.dot(a, b, preferred_element_type=F32)


def _mm_nt(a, b):
    return lax.dot_general(a, b, (((1,), (1,)), ((), ())), preferred_element_type=F32)


def _split_bf16(x, n):
    parts = [x.astype(BF16)]
    for _ in range(n - 1):
        x = x - parts[-1].astype(F32)
        parts.append(x.astype(BF16))
    return parts


def _rmsnorm_rows(h, gain):
    ms = jnp.mean(h * h, axis=-1, keepdims=True)
    return h * lax.rsqrt(ms + RMS_EPS) * gain


def _log_sigmoid(z):
    return jnp.minimum(z, 0.0) - jnp.log(1.0 + jnp.exp(-jnp.abs(z)))


def _params(n_grid_axes):
    return pltpu.CompilerParams(dimension_semantics=("arbitrary",) * n_grid_axes,
                                vmem_limit_bytes=VMEM_LIMIT)


def _ffn_kernel(*refs, meta_row, emit_norm):
    refs = list(refs)
    x_ref = refs.pop(0)
    meta_ref = refs.pop(0) if meta_row is not None else None
    gain_ref, wg_ref, wu_ref, wd_ref = refs[:4]
    refs = refs[4:]
    if emit_norm:
        ngain_ref, wf_ref, bf_ref, o_ref, n_ref, lf_ref, xn_ref = refs
    else:
        o_ref, xn_ref = refs
    i = pl.program_id(0)
    j = pl.program_id(1)

    @pl.when(j == 0)
    def _():
        if meta_row is None:
            o_ref[...] = x_ref[...]
        else:
            last = pl.num_programs(0) - 1

            @pl.when(i < last)
            def _():
                o_ref[...] = x_ref[...]

            @pl.when(i == last)
            def _():
                o_ref[0:meta_row, :] = x_ref[0:meta_row, :]
                o_ref[meta_row:meta_row + N_META, :] = meta_ref[...]
        xn_ref[...] = _rmsnorm_rows(o_ref[...], gain_ref[...]).astype(BF16)

    xn = xn_ref[...]
    g = _mm(xn, wg_ref[...].astype(BF16))
    u = _mm(xn, wu_ref[...].astype(BF16))
    a = (g * jax.nn.sigmoid(g)) * (u * FFN_RESIDUAL_WEIGHT)
    o_ref[...] += _mm(a.astype(BF16), wd_ref[...].astype(BF16))

    if emit_norm:
        @pl.when(j == pl.num_programs(1) - 1)
        def _():
            n = _rmsnorm_rows(o_ref[...], ngain_ref[...]).astype(BF16)
            n_ref[...] = n
            lf_ref[...] = _log_sigmoid(_mm(n, wf_ref[...].astype(BF16)) + bf_ref[...]) * LOG2E


def _ffn(x, meta, gain, wg, wu, wd, *, rows, tm, next_gain=None, forget=None):
    ni = rows // tm
    nj = D_FF // FFN_TILE_F
    emit_norm = next_gain is not None
    in_specs = [pl.BlockSpec((tm, D_MODEL), lambda i, j: (i, 0))]
    args = [x]
    meta_row = None
    if meta is not None:
        meta_row = x.shape[0] - (ni - 1) * tm
        assert meta_row + N_META == tm
        in_specs.append(pl.BlockSpec((N_META, D_MODEL), lambda i, j: (0, 0)))
        args.append(meta)
    in_specs += [
        pl.BlockSpec((1, D_MODEL), lambda i, j: (0, 0)),
        pl.BlockSpec((D_MODEL, FFN_TILE_F), lambda i, j: (0, j)),
        pl.BlockSpec((D_MODEL, FFN_TILE_F), lambda i, j: (0, j)),
        pl.BlockSpec((FFN_TILE_F, D_MODEL), lambda i, j: (j, 0)),
    ]
    args += [gain.reshape(1, D_MODEL), wg, wu, wd]
    out_shape = [jax.ShapeDtypeStruct((rows, D_MODEL), F32)]
    out_specs = [pl.BlockSpec((tm, D_MODEL), lambda i, j: (i, 0))]
    if emit_norm:
        w_f, b_f = forget
        in_specs += [pl.BlockSpec((1, D_MODEL), lambda i, j: (0, 0)),
                     pl.BlockSpec((D_MODEL, LANES), lambda i, j: (0, 0)),
                     pl.BlockSpec((1, LANES), lambda i, j: (0, 0))]
        args += [next_gain.reshape(1, D_MODEL), w_f, b_f]
        out_shape += [jax.ShapeDtypeStruct((rows, D_MODEL), BF16), jax.ShapeDtypeStruct((rows, LANES), F32)]
        out_specs += [pl.BlockSpec((tm, D_MODEL), lambda i, j: (i, 0)), pl.BlockSpec((tm, LANES), lambda i, j: (i, 0))]
    res = pl.pallas_call(
        functools.partial(_ffn_kernel, meta_row=meta_row, emit_norm=emit_norm),
        grid=(ni, nj),
        in_specs=in_specs,
        out_specs=out_specs,
        out_shape=out_shape,
        scratch_shapes=[pltpu.VMEM((tm, D_MODEL), BF16)],
        compiler_params=_params(2),
        name="ffn_norm" if emit_norm else "ffn",
    )(*args)
    return res if emit_norm else res[0]


def _proj_kernel(n_ref, w_ref, *refs, epilogue):
    p_ref = refs[0] if len(refs) == 3 else None
    o_ref, wb_ref = refs[-2:]

    @pl.when(pl.program_id(1) == 0)
    def _():
        wb_ref[...] = w_ref[...].T.astype(BF16)

    def run(fn):
        for rows in _row_chunks(n_ref.shape[0]):
            o_ref[rows, :] = fn(_mm(n_ref[rows, :], wb_ref[...])).astype(o_ref.dtype)

    epilogue(run, p_ref, pl.program_id(0))


def _row_chunks(tm):
    n = next(n for n in (4, 3, 2, 1) if tm % (n * BF16_ROWS) == 0)
    return [slice(c * (tm // n), (c + 1) * (tm // n)) for c in range(n)]


def _fox_epilogue(run, p_ref, j):
    def head_norm(y):
        heads = [slice(h * HEAD_DIM, (h + 1) * HEAD_DIM) for h in range(y.shape[1] // HEAD_DIM)]
        return jnp.concatenate([_rmsnorm_rows(y[:, sl], p_ref[:, sl]) for sl in heads], axis=1)

    pl.when(j < 2)(lambda: run(head_norm))
    pl.when(j >= 2)(lambda: run(lambda y: y))


def _scale_epilogue(run, p_ref, j):
    run(lambda y: y * p_ref[...])


def _gate_epilogue(run, p_ref, j):
    run(jax.nn.sigmoid)


def _proj(n, wt, row0, n_cols, p, *, rows, tm, tn, epilogue, out_dtype, name):
    assert n_cols % tn == 0
    nj, ni = n_cols // tn, rows // tm
    if row0 % tn == 0:
        w_spec = pl.BlockSpec((tn, D_MODEL), lambda j, i: (row0 // tn + j, 0))
    else:
        assert row0 % SUBLANES == 0 and tn % SUBLANES == 0
        w_spec = pl.BlockSpec((pl.Element(tn), pl.Element(D_MODEL)),
                              lambda j, i: ((row0 // SUBLANES + j * (tn // SUBLANES)) * SUBLANES, 0))
    in_specs = [pl.BlockSpec((tm, D_MODEL), lambda j, i: (i, 0)), w_spec]
    args = [n, wt]
    if p is not None:
        in_specs.append(pl.BlockSpec((1, tn), lambda j, i: (0, j)))
        args.append(p)
    return pl.pallas_call(
        functools.partial(_proj_kernel, epilogue=epilogue),
        grid=(nj, ni),
        in_specs=in_specs,
        out_specs=pl.BlockSpec((tm, tn), lambda j, i: (i, j)),
        out_shape=jax.ShapeDtypeStruct((rows, n_cols), out_dtype),
        scratch_shapes=[pltpu.VMEM((D_MODEL, tn), BF16)],
        compiler_params=_params(2),
        name=name,
    )(*args)


def _decay_kernel(lf_ref, qc_ref, kc_ref, c_ref, *, n_batch, seq):
    T = DECAY_TERMS
    n_real = n_batch * seq
    r_i = lax.broadcasted_iota(jnp.int32, (LANES, LANES), 0)
    c_i = lax.broadcasted_iota(jnp.int32, (LANES, LANES), 1)
    tri = jnp.where(r_i >= c_i, 1.0, 0.0).astype(BF16)

    def cum_block(x, carry):
        acc = carry
        for p in _split_bf16(x, T):
            acc = acc + _mm(tri, p)
        return acc

    xm = jnp.concatenate([lf_ref[n_real:n_real + N_META, :], jnp.zeros((LANES - N_META, LANES), F32)], axis=0)
    cm = cum_block(xm, jnp.zeros((1, LANES), F32))
    c_ref[n_real:n_real + N_META, :] = cm[:N_META, :]
    base = cm[N_META - 1:N_META, :]

    def body(t, carries):
        out = []
        for b, carry in enumerate(carries):
            r0 = pl.multiple_of(b * seq + t * LANES, LANES)
            c = cum_block(lf_ref[pl.ds(r0, LANES), :], carry)
            c_ref[pl.ds(r0, LANES), :] = c
            out.append(c[LANES - 1:LANES, :])
        return tuple(out)
    lax.fori_loop(0, seq // LANES, body, (base,) * n_batch)

    k_i = lax.broadcasted_iota(jnp.int32, (T * LANES, LANES), 0)
    n_i = lax.broadcasted_iota(jnp.int32, (T * LANES, LANES), 1)
    sel_q = jnp.zeros((T * LANES, LANES), F32)
    sel_k = jnp.zeros((T * LANES, LANES), F32)
    lane = lax.broadcasted_iota(jnp.int32, (1, LANES), 1)
    ones_q = jnp.zeros((1, LANES), F32)
    ones_k = jnp.zeros((1, LANES), F32)
    for h in range(N_HEADS):
        for p in range(T):
            row = p * LANES + h
            sel_q = jnp.where((k_i == row) & (n_i == 2 * T * h + p), 1.0, sel_q)
            sel_k = jnp.where((k_i == row) & (n_i == 2 * T * h + T + p), -1.0, sel_k)
            ones_q = jnp.where(lane == 2 * T * h + T + p, 1.0, ones_q)
            ones_k = jnp.where(lane == 2 * T * h + p, 1.0, ones_k)
    sel_q = sel_q.astype(BF16)
    sel_k = sel_k.astype(BF16)

    def expand(c):
        pieces = jnp.concatenate(_split_bf16(c, T), axis=1)
        return ((_mm(pieces, sel_q) + ones_q).astype(BF16), (_mm(pieces, sel_k) + ones_k).astype(BF16))

    chunk = ROW_TILE_REAL

    def ebody(t, _):
        r0 = pl.multiple_of(t * chunk, chunk)
        q, k = expand(c_ref[pl.ds(r0, chunk), :])
        qc_ref[pl.ds(r0, chunk), :] = q
        kc_ref[pl.ds(r0, chunk), :] = k
        return 0
    lax.fori_loop(0, n_real // chunk, ebody, 0)
    _, km = expand(c_ref[n_real:n_real + N_META, :])
    kc_ref[n_real:n_real + N_META, :] = km


def _decay(lf, *, n_batch, seq):
    n_real = n_batch * seq
    return pl.pallas_call(
        functools.partial(_decay_kernel, n_batch=n_batch, seq=seq),
        out_shape=[jax.ShapeDtypeStruct((n_real, LANES), BF16),
                   jax.ShapeDtypeStruct((n_real + N_META, LANES), BF16)],
        scratch_shapes=[pltpu.VMEM((n_real + N_META, LANES), F32)],
        compiler_params=pltpu.CompilerParams(vmem_limit_bytes=VMEM_LIMIT),
        name="decay",
    )(lf)


def _eye(n):
    return jnp.where(lax.broadcasted_iota(jnp.int32, (n, n), 0) == lax.broadcasted_iota(jnp.int32, (n, n), 1),
                     1.0, 0.0).astype(BF16)


def _stage_keys_values(kpad_ref, vt_ref, meta_k_refs, v_ref, mv_ref, *, seq):
    eye = _eye(HEAD_DIM)
    kpad_ref[...] = jnp.zeros_like(kpad_ref)
    for n, ref in enumerate(meta_k_refs):
        kpad_ref[0:N_META, n * HEAD_DIM:(n + 1) * HEAD_DIM] = ref[...]
    vpad = jnp.concatenate([mv_ref[...], jnp.zeros((LANES - N_META, HEAD_DIM), BF16)], axis=0)
    vt_ref[:, seq:seq + LANES] = _mm_nt(eye, vpad).astype(BF16)
    chunk = 2 * ATT_SUB
    for c0 in range(0, seq, chunk):
        vt_ref[:, c0:c0 + chunk] = _mm_nt(eye, v_ref[c0:c0 + chunk, :]).astype(BF16)


def _store_transposed(o_ref, r0, acc_t):
    o_ref[pl.ds(r0, ATT_SUB), :] = acc_t.T.astype(o_ref.dtype)


def _fox_head(q_ref, qc_ref, k_ref, kc_ref, v_ref, mk_ref, mkc_ref, mv_ref, o_ref, kpad_ref, vt_ref,
              sa_ref, sb_ref, *, seq):
    T = DECAY_TERMS
    h = pl.program_id(1)
    _stage_keys_values(kpad_ref, vt_ref, (mk_ref, mkc_ref), v_ref, mv_ref, seq=seq)

    lane = lax.broadcasted_iota(jnp.int32, (1, LANES), 1)
    head_lanes = (lane >= 2 * T * h) & (lane < 2 * T * (h + 1))
    meta_valid = lax.broadcasted_iota(jnp.int32, (LANES, ATT_SUB), 0) < N_META
    causal = (lax.broadcasted_iota(jnp.int32, (ATT_SUB, ATT_SUB), 0)
              <= lax.broadcasted_iota(jnp.int32, (ATT_SUB, ATT_SUB), 1))
    all_subs = tuple(range(ATT_NSUB))

    def keys(c0):
        return jnp.concatenate([k_ref[pl.ds(c0, ATT_SUB), :], kc_ref[pl.ds(c0, ATT_SUB), :]], axis=1)

    def online(s, vt, m, l, acc):
        m_new = jnp.maximum(m, jnp.max(s, axis=0, keepdims=True))
        alpha = jnp.exp2(m - m_new)
        p = jnp.exp2(s - m_new)
        l = alpha * l + jnp.sum(p, axis=0, keepdims=True)
        acc = alpha * acc + _mm(vt, p.astype(BF16))
        return m_new, l, acc

    def block(i):
        r0 = pl.multiple_of(i * ATT_TQ, ATT_TQ)
        qx = jnp.where(head_lanes, qc_ref[pl.ds(r0, ATT_TQ), :], jnp.zeros((), BF16))
        qa = jnp.concatenate([q_ref[pl.ds(r0, ATT_TQ), :], qx], axis=1)
        subs = [qa[n * ATT_SUB:(n + 1) * ATT_SUB] for n in all_subs]

        def issue(buf_ref, c0, which=all_subs):
            ka = keys(c0)
            for n in which:
                buf_ref[n] = _mm_nt(ka, subs[n])

        def consume(buf_ref, c0, state, masked=None, which=all_subs):
            vt = vt_ref[:, pl.ds(c0, ATT_SUB)]
            out = list(state)
            for n in which:
                s = buf_ref[n]
                if n == masked:
                    s = jnp.where(causal, s, -jnp.inf)
                out[n] = online(s, vt, *out[n])
            return tuple(out)

        issue(sa_ref, 0)

        state = []
        for qs in subs:
            s = jnp.where(meta_valid, _mm_nt(kpad_ref[...], qs), -jnp.inf)
            m = jnp.max(s, axis=0, keepdims=True)
            p = jnp.exp2(s - m)
            state.append((m, jnp.sum(p, axis=0, keepdims=True), _mm(vt_ref[:, seq:seq + LANES], p.astype(BF16))))

        def kv_pair(t, state):
            c0 = pl.multiple_of(t * (2 * ATT_SUB), 2 * ATT_SUB)
            c1 = pl.multiple_of(c0 + ATT_SUB, ATT_SUB)
            issue(sb_ref, c1)
            state = consume(sa_ref, c0, state)
            issue(sa_ref, pl.multiple_of(c0 + 2 * ATT_SUB, 2 * ATT_SUB))
            return consume(sb_ref, c1, state)

        def finish(state):
            bufs = (sa_ref, sb_ref)
            for kb in all_subs:
                if kb + 1 < ATT_NSUB:
                    issue(bufs[(kb + 1) % 2], pl.multiple_of(r0 + (kb + 1) * ATT_SUB, ATT_SUB), all_subs[kb + 1:])
                state = consume(bufs[kb % 2], pl.multiple_of(r0 + kb * ATT_SUB, ATT_SUB), state, kb, all_subs[kb:])
            for n, (m, l, acc) in enumerate(state):
                _store_transposed(o_ref, pl.multiple_of(r0 + n * ATT_SUB, ATT_SUB), acc / l)

        return tuple(state), kv_pair, finish

    return block


def _sb_head(q_ref, k_ref, v_ref, mk_ref, mv_ref, o_ref, kpad_ref, vt_ref, *, seq):
    _stage_keys_values(kpad_ref, vt_ref, (mk_ref,), v_ref, mv_ref, seq=seq)

    def later_keys(n):
        return jnp.where(lax.broadcasted_iota(jnp.int32, (n, n), 0) < lax.broadcasted_iota(jnp.int32, (n, n), 1),
                         1.0, 0.0).astype(BF16)

    u_blk = later_keys(ATT_SUB)
    u_meta = later_keys(LANES)
    meta_valid = lax.broadcasted_iota(jnp.int32, (LANES, ATT_SUB), 0) < N_META
    strict = (lax.broadcasted_iota(jnp.int32, (ATT_SUB, ATT_SUB), 0)
              < lax.broadcasted_iota(jnp.int32, (ATT_SUB, ATT_SUB), 1))
    all_subs = tuple(range(ATT_NSUB))

    def steps(items, states):
        states = list(states)
        zs = [_mm_nt(kblk, qs) for _, qs, kblk, _, _, _ in items]
        own, cums = [], []
        for z, (_, _, _, _, u, mask) in zip(zs, items):
            sp = jnp.maximum(jnp.log2(1.0 + jnp.exp2(jnp.minimum(z, MAX_EXP2))), z)
            if mask is not None:
                sp = jnp.where(mask, sp, 0.0)
            pieces = _split_bf16(sp, SUFFIX_TERMS)
            own.append((z - sp, sp[0:1, :]))
            cums.append(functools.reduce(lambda a, b: a + b, [_mm(u, p) for p in pieces]))
        for (log_beta, sp_first), cum, (n, _, _, vt, _, mask) in zip(own, cums, items):
            r, acc = states[n]
            w = jnp.exp2(log_beta - cum - r)
            if mask is not None:
                w = jnp.where(mask, w, 0.0)
            states[n] = (r + (cum[0:1, :] + sp_first), acc + _mm(vt, w.astype(BF16)))
        return tuple(states)

    def block(i):
        r0 = pl.multiple_of(i * ATT_TQ, ATT_TQ)
        q = [q_ref[pl.ds(pl.multiple_of(r0 + n * ATT_SUB, ATT_SUB), ATT_SUB), :] for n in all_subs]
        zero = (jnp.zeros((1, ATT_SUB), F32), jnp.zeros((HEAD_DIM, ATT_SUB), F32))

        def item(n, c0, mask):
            return (n, q[n], k_ref[pl.ds(c0, ATT_SUB), :], vt_ref[:, pl.ds(c0, ATT_SUB)], u_blk, mask)

        def pair_items(jj):
            items = []
            for d in (1, 2):
                c0 = pl.multiple_of(r0 - (2 * jj + d) * ATT_SUB, ATT_SUB)
                items += [item(n, c0, None) for n in all_subs]
            return items

        diag = []
        for kb in reversed(all_subs):
            c0 = pl.multiple_of(r0 + kb * ATT_SUB, ATT_SUB)
            diag += [item(n, c0, strict if n == kb else None) for n in all_subs[kb:]]
        states = steps(diag, (zero,) * ATT_NSUB)

        def kv_pair(jj, states):
            return steps(pair_items(jj), states)

        def finish(states):
            meta = [(n, q[n], kpad_ref[...], vt_ref[:, seq:seq + LANES], u_meta, meta_valid) for n in all_subs]
            for n, (_, acc) in enumerate(steps(meta, states)):
                _store_transposed(o_ref, pl.multiple_of(r0 + n * ATT_SUB, ATT_SUB), acc)

        return states, kv_pair, finish

    return block


def _attention_kernel(fq, fqc, fk, fkc, fv, fmk, fmkc, fmv, sq, sk, sv, smk, smv, of_ref, os_ref,
                      f_kpad, f_vt, f_sa, f_sb, s_kpad, s_vt, *, seq):
    fox = _fox_head(fq, fqc, fk, fkc, fv, fmk, fmkc, fmv, of_ref, f_kpad, f_vt, f_sa, f_sb, seq=seq)
    stick = _sb_head(sq, sk, sv, smk, smv, os_ref, s_kpad, s_vt, seq=seq)

    def q_block(i, _):
        f_state, f_pair, f_finish = fox(i)
        s_state, s_pair, s_finish = stick(i)

        def kv_pair(t, carry):
            return f_pair(t, carry[0]), s_pair(t, carry[1])

        f_state, s_state = lax.fori_loop(0, i * (ATT_NSUB // 2), kv_pair, (f_state, s_state))
        f_finish(f_state)
        s_finish(s_state)
        return 0

    lax.fori_loop(0, seq // ATT_TQ, q_block, 0)


def _head_spec(rows, col_block0):
    return pl.BlockSpec((rows, HEAD_DIM), lambda b, h: (b, col_block0 + h))


def _meta_spec(n_real, col_block0):
    return pl.BlockSpec((N_META, HEAD_DIM), lambda b, h: (n_real // N_META, col_block0 + h))


def _attention(fox_qkv, qc, kc, sb_qkv, *, n_batch, seq):
    n_real = n_batch * seq
    shared = pl.BlockSpec((seq, LANES), lambda b, h: (b, 0))
    qkv_specs = [_head_spec(seq, 0), _head_spec(seq, N_HEADS), _head_spec(seq, 2 * N_HEADS)]
    meta_specs = [_meta_spec(n_real, N_HEADS), _meta_spec(n_real, 2 * N_HEADS)]
    score_buf = pltpu.VMEM((ATT_NSUB, ATT_SUB, ATT_SUB), F32)
    return pl.pallas_call(
        functools.partial(_attention_kernel, seq=seq),
        grid=(n_batch, N_HEADS),
        in_specs=[qkv_specs[0], shared, qkv_specs[1], shared, qkv_specs[2],
                  meta_specs[0], pl.BlockSpec((N_META, LANES), lambda b, h: (n_real // N_META, 0)), meta_specs[1],
                  *qkv_specs, *meta_specs],
        out_specs=[_head_spec(seq, 0), _head_spec(seq, 0)],
        out_shape=[jax.ShapeDtypeStruct((n_real, WIDTH), BF16)] * 2,
        scratch_shapes=[pltpu.VMEM((LANES, 2 * HEAD_DIM), BF16), pltpu.VMEM((HEAD_DIM, seq + LANES), BF16),
                        score_buf, score_buf,
                        pltpu.VMEM((LANES, HEAD_DIM), BF16), pltpu.VMEM((HEAD_DIM, seq + LANES), BF16)],
        compiler_params=_params(2),
        name="attention",
    )(fox_qkv, qc, fox_qkv, kc, fox_qkv, fox_qkv, kc, fox_qkv, sb_qkv, sb_qkv, sb_qkv, sb_qkv, sb_qkv)


def _mix_merge_kernel(of_ref, os_ref, gf_ref, gs_ref, wbf_ref, wbs_ref, o_ref, wbf_b, wbs_b):
    @pl.when(pl.program_id(1) == 0)
    def _():
        wbf_b[...] = wbf_ref[...].astype(BF16)
        wbs_b[...] = wbs_ref[...].astype(BF16)

    for rows in _row_chunks(of_ref.shape[0]):
        tf = _mm(of_ref[rows, :], wbf_b[...])
        ts = _mm(os_ref[rows, :], wbs_b[...])
        merged = gf_ref[rows, :].astype(F32) * tf + gs_ref[rows, :].astype(F32) * ts
        o_ref[rows, :] = merged.astype(o_ref.dtype)


def _mix_merge(o_fox, o_sb, gates, wbf, wbs, *, rows, tm, tn):
    ni, nj = rows // tm, D_MODEL // tn
    return pl.pallas_call(
        _mix_merge_kernel,
        grid=(nj, ni),
        in_specs=[
            pl.BlockSpec((tm, WIDTH), lambda j, i: (i, 0)),
            pl.BlockSpec((tm, WIDTH), lambda j, i: (i, 0)),
            pl.BlockSpec((tm, tn), lambda j, i: (i, j)),
            pl.BlockSpec((tm, tn), lambda j, i: (i, nj + j)),
            pl.BlockSpec((WIDTH, tn), lambda j, i: (0, j)),
            pl.BlockSpec((WIDTH, tn), lambda j, i: (0, j)),
        ],
        out_specs=pl.BlockSpec((tm, tn), lambda j, i: (i, j)),
        out_shape=jax.ShapeDtypeStruct((rows, D_MODEL), BF16),
        scratch_shapes=[pltpu.VMEM((WIDTH, tn), BF16), pltpu.VMEM((WIDTH, tn), BF16)],
        compiler_params=_params(2),
        name="mix_merge",
    )(o_fox, o_sb, gates, gates, wbf, wbs)


def _mix_proj_kernel(m_ref, w_ref, h_ref, o_ref, wb_ref):
    @pl.when(pl.program_id(1) == 0)
    def _():
        wb_ref[...] = w_ref[...].astype(BF16)

    for rows in _row_chunks(m_ref.shape[0]):
        o_ref[rows, :] = h_ref[rows, :] + _mm(m_ref[rows, :], wb_ref[...])


def _mix_proj(merged, wo, h, *, rows, tm, tn):
    ni, nj = rows // tm, D_MODEL // tn
    return pl.pallas_call(
        _mix_proj_kernel,
        grid=(nj, ni),
        in_specs=[
            pl.BlockSpec((tm, D_MODEL), lambda j, i: (i, 0)),
            pl.BlockSpec((D_MODEL, tn), lambda j, i: (0, j)),
            pl.BlockSpec((tm, tn), lambda j, i: (i, j)),
        ],
        out_specs=pl.BlockSpec((tm, tn), lambda j, i: (i, j)),
        out_shape=jax.ShapeDtypeStruct((rows, D_MODEL), F32),
        scratch_shapes=[pltpu.VMEM((D_MODEL, tn), BF16)],
        compiler_params=_params(2),
        name="mix_proj",
    )(merged, wo, h)


def kernel(x, meta_tokens, ffn1_norm, ffn1_w_gate, ffn1_w_up, ffn1_w_down, mix_norm, w_in, b_forget, fox_q_norm, fox_k_norm, w_branch_fox, w_branch_sb, w_out, ffn2_norm, ffn2_w_gate, ffn2_w_up, ffn2_w_down):
    n_batch, seq, _ = x.shape
    n_real = n_batch * seq
    n_all = n_real + N_META
    assert ffn1_norm.shape[0] == 1
    assert n_all % ROW_TILE_ALL == 0 and n_real % ROW_TILE_REAL == 0 and seq % ATT_TQ == 0

    wt = w_in[0].T
    r_f = 3 * WIDTH
    r_sb = r_f + N_HEADS
    r_gate = r_sb + 3 * WIDTH
    w_f = jnp.pad(w_in[0][:, r_f:r_sb], ((0, 0), (0, LANES - N_HEADS)))
    b_f = jnp.pad(b_forget[0].reshape(1, N_HEADS), ((0, 0), (0, LANES - N_HEADS)))
    fox_gain = jnp.concatenate([fox_q_norm[0].reshape(1, WIDTH) * (SCALE * LOG2E), fox_k_norm[0].reshape(1, WIDTH),
                                jnp.ones((1, WIDTH), F32)], axis=1)
    sb_scale = jnp.concatenate([jnp.full((1, WIDTH), SCALE * LOG2E, F32), jnp.ones((1, 2 * WIDTH), F32)], axis=1)

    h1, n1, log_f = _ffn(x.reshape(n_real, D_MODEL), meta_tokens.astype(F32), ffn1_norm[0], ffn1_w_gate[0],
                         ffn1_w_up[0], ffn1_w_down[0], rows=n_all, tm=ROW_TILE_ALL, next_gain=mix_norm[0],
                         forget=(w_f, b_f))

    proj = functools.partial(_proj, n1, rows=n_all, tm=ROW_TILE_ALL)
    fox_qkv = proj(wt, 0, 3 * WIDTH, fox_gain, tn=PROJ_TILE_N, epilogue=_fox_epilogue,
                   out_dtype=BF16, name="proj_fox")
    sb_qkv = proj(wt, r_sb, 3 * WIDTH, sb_scale, tn=PROJ_TILE_N, epilogue=_scale_epilogue,
                  out_dtype=BF16, name="proj_sb")
    gates = _proj(n1, wt, r_gate, 2 * D_MODEL, None, rows=n_real, tm=ROW_TILE_REAL,
                  tn=PROJ_TILE_N, epilogue=_gate_epilogue, out_dtype=BF16, name="proj_gates")

    qc, kc = _decay(log_f, n_batch=n_batch, seq=seq)
    o_fox, o_sb = _attention(fox_qkv, qc, kc, sb_qkv, n_batch=n_batch, seq=seq)

    merged = _mix_merge(o_fox, o_sb, gates, w_branch_fox[0], w_branch_sb[0],
                        rows=n_real, tm=ROW_TILE_REAL, tn=PROJ_TILE_N)
    h2 = _mix_proj(merged, w_out[0], h1, rows=n_real, tm=ROW_TILE_REAL, tn=PROJ_TILE_N)
    h3 = _ffn(h2, None, ffn2_norm[0], ffn2_w_gate[0], ffn2_w_up[0], ffn2_w_down[0],
              rows=n_real, tm=ROW_TILE_REAL)
    return h3.reshape(n_batch, seq, D_MODEL)
```

```python
import functools
import math

import jax
import jax.numpy as jnp
from jax import lax
from jax.experimental import pallas as pl
from jax.experimental.pallas import tpu as pltpu

F32 = jnp.float32
BF16 = jnp.bfloat16

D_MODEL = 2048
D_FF = 5632
N_META = 16
HEAD_DIM = 128
N_HEADS = 8
WIDTH = N_HEADS * HEAD_DIM
RMS_EPS = 1e-6
FFN_RESIDUAL_WEIGHT = 0.5
SCALE = HEAD_DIM ** -0.5
LOG2E = math.log2(math.e)

LANES = 128
SUBLANES = 8
BF16_ROWS = 16
VMEM_LIMIT = 58 * 2**20

ROW_TILE_ALL = 912
ROW_TILE_REAL = 1024
FFN_TILE_F = 256
PROJ_TILE_N = 1024
ATT_SUB = 256
ATT_NSUB = 4
ATT_TQ = ATT_SUB * ATT_NSUB
DECAY_TERMS = 3
SUFFIX_TERMS = 1
MAX_EXP2 = 126.0


def _mm(a, b):
    return jnp.dot(a, b, preferred_element_type=F32)


def _mm_nt(a, b):
    return lax.dot_general(a, b, (((1,), (1,)), ((), ())), preferred_element_type=F32)


def _split_bf16(x, n):
    parts = [x.astype(BF16)]
    for _ in range(n - 1):
        x = x - parts[-1].astype(F32)
        parts.append(x.astype(BF16))
    return parts


def _rmsnorm_rows(h, gain):
    ms = jnp.mean(h * h, axis=-1, keepdims=True)
    return h * lax.rsqrt(ms + RMS_EPS) * gain


def _log_sigmoid(z):
    return jnp.minimum(z, 0.0) - jnp.log(1.0 + jnp.exp(-jnp.abs(z)))


def _params(n_grid_axes):
    return pltpu.CompilerParams(dimension_semantics=("arbitrary",) * n_grid_axes,
                                vmem_limit_bytes=VMEM_LIMIT)


def _ffn_kernel(*refs, meta_row, emit_norm):
    refs = list(refs)
    x_ref = refs.pop(0)
    meta_ref = refs.pop(0) if meta_row is not None else None
    gain_ref, wg_ref, wu_ref, wd_ref = refs[:4]
    refs = refs[4:]
    if emit_norm:
        ngain_ref, wf_ref, bf_ref, o_ref, n_ref, lf_ref, xn_ref = refs
    else:
        o_ref, xn_ref = refs
    i = pl.program_id(0)
    j = pl.program_id(1)

    @pl.when(j == 0)
    def _():
        if meta_row is None:
            h = x_ref[...]
            o_ref[...] = h
            xn_ref[...] = _rmsnorm_rows(h, gain_ref[...]).astype(BF16)
        else:
            last = pl.num_programs(0) - 1

            @pl.when(i < last)
            def _():
                o_ref[...] = x_ref[...]

            @pl.when(i == last)
            def _():
                o_ref[0:meta_row, :] = x_ref[0:meta_row, :]
                o_ref[meta_row:meta_row + N_META, :] = meta_ref[...]
            xn_ref[...] = _rmsnorm_rows(o_ref[...], gain_ref[...]).astype(BF16)

    xn = xn_ref[...]
    g = _mm(xn, wg_ref[...].astype(BF16))
    u = _mm(xn, wu_ref[...].astype(BF16))
    a = (g * jax.nn.sigmoid(g)) * (u * FFN_RESIDUAL_WEIGHT)
    o_ref[...] += _mm(a.astype(BF16), wd_ref[...].astype(BF16))

    if emit_norm:
        @pl.when(j == pl.num_programs(1) - 1)
        def _():
            n = _rmsnorm_rows(o_ref[...], ngain_ref[...]).astype(BF16)
            n_ref[...] = n
            lf_ref[...] = _log_sigmoid(_mm(n, wf_ref[...].astype(BF16)) + bf_ref[...]) * LOG2E


def _ffn(x, meta, gain, wg, wu, wd, *, rows, tm, next_gain=None, forget=None):
    ni = rows // tm
    nj = D_FF // FFN_TILE_F
    emit_norm = next_gain is not None
    in_specs = [pl.BlockSpec((tm, D_MODEL), lambda i, j: (i, 0))]
    args = [x]
    meta_row = None
    if meta is not None:
        meta_row = x.shape[0] - (ni - 1) * tm
        assert meta_row + N_META == tm
        in_specs.append(pl.BlockSpec((N_META, D_MODEL), lambda i, j: (0, 0)))
        args.append(meta)
    in_specs += [
        pl.BlockSpec((1, D_MODEL), lambda i, j: (0, 0)),
        pl.BlockSpec((D_MODEL, FFN_TILE_F), lambda i, j: (0, j)),
        pl.BlockSpec((D_MODEL, FFN_TILE_F), lambda i, j: (0, j)),
        pl.BlockSpec((FFN_TILE_F, D_MODEL), lambda i, j: (j, 0)),
    ]
    args += [gain.reshape(1, D_MODEL), wg, wu, wd]
    out_shape = [jax.ShapeDtypeStruct((rows, D_MODEL), F32)]
    out_specs = [pl.BlockSpec((tm, D_MODEL), lambda i, j: (i, 0))]
    if emit_norm:
        w_f, b_f = forget
        in_specs += [pl.BlockSpec((1, D_MODEL), lambda i, j: (0, 0)),
                     pl.BlockSpec((D_MODEL, LANES), lambda i, j: (0, 0)),
                     pl.BlockSpec((1, LANES), lambda i, j: (0, 0))]
        args += [next_gain.reshape(1, D_MODEL), w_f, b_f]
        out_shape += [jax.ShapeDtypeStruct((rows, D_MODEL), BF16), jax.ShapeDtypeStruct((rows, LANES), F32)]
        out_specs += [pl.BlockSpec((tm, D_MODEL), lambda i, j: (i, 0)), pl.BlockSpec((tm, LANES), lambda i, j: (i, 0))]
    res = pl.pallas_call(
        functools.partial(_ffn_kernel, meta_row=meta_row, emit_norm=emit_norm),
        grid=(ni, nj),
        in_specs=in_specs,
        out_specs=out_specs,
        out_shape=out_shape,
        scratch_shapes=[pltpu.VMEM((tm, D_MODEL), BF16)],
        compiler_params=_params(2),
        name="ffn_norm" if emit_norm else "ffn",
    )(*args)
    return res if emit_norm else res[0]


def _proj_kernel(n_ref, w_ref, *refs, epilogue):
    p_ref = refs[0] if len(refs) == 3 else None
    o_ref, wb_ref = refs[-2:]

    @pl.when(pl.program_id(1) == 0)
    def _():
        wb_ref[...] = w_ref[...].T.astype(BF16)

    def run(fn):
        for rows in _row_chunks(n_ref.shape[0]):
            o_ref[rows, :] = fn(_mm(n_ref[rows, :], wb_ref[...])).astype(o_ref.dtype)

    epilogue(run, p_ref, pl.program_id(0))


def _row_chunks(tm):
    n = next(n for n in (4, 3, 2, 1) if tm % (n * BF16_ROWS) == 0)
    return [slice(c * (tm // n), (c + 1) * (tm // n)) for c in range(n)]


def _fox_epilogue(run, p_ref, j):
    def head_norm(y):
        heads = [slice(h * HEAD_DIM, (h + 1) * HEAD_DIM) for h in range(y.shape[1] // HEAD_DIM)]
        return jnp.concatenate([_rmsnorm_rows(y[:, sl], p_ref[:, sl]) for sl in heads], axis=1)

    pl.when(j < 2)(lambda: run(head_norm))
    pl.when(j >= 2)(lambda: run(lambda y: y))


def _scale_epilogue(run, p_ref, j):
    run(lambda y: y * p_ref[...])


def _gate_epilogue(run, p_ref, j):
    run(jax.nn.sigmoid)


def _proj(n, wt, row0, n_cols, p, *, rows, tm, tn, epilogue, out_dtype, name):
    assert n_cols % tn == 0
    nj, ni = n_cols // tn, rows // tm
    if row0 % tn == 0:
        w_spec = pl.BlockSpec((tn, D_MODEL), lambda j, i: (row0 // tn + j, 0))
    else:
        assert row0 % SUBLANES == 0 and tn % SUBLANES == 0
        w_spec = pl.BlockSpec((pl.Element(tn), pl.Element(D_MODEL)),
                              lambda j, i: ((row0 // SUBLANES + j * (tn // SUBLANES)) * SUBLANES, 0))
    in_specs = [pl.BlockSpec((tm, D_MODEL), lambda j, i: (i, 0)), w_spec]
    args = [n, wt]
    if p is not None:
        in_specs.append(pl.BlockSpec((1, tn), lambda j, i: (0, j)))
        args.append(p)
    return pl.pallas_call(
        functools.partial(_proj_kernel, epilogue=epilogue),
        grid=(nj, ni),
        in_specs=in_specs,
        out_specs=pl.BlockSpec((tm, tn), lambda j, i: (i, j)),
        out_shape=jax.ShapeDtypeStruct((rows, n_cols), out_dtype),
        scratch_shapes=[pltpu.VMEM((D_MODEL, tn), BF16)],
        compiler_params=_params(2),
        name=name,
    )(*args)


def _decay_kernel(lf_ref, qc_ref, kc_ref, c_ref, *, n_batch, seq):
    T = DECAY_TERMS
    n_real = n_batch * seq
    r_i = lax.broadcasted_iota(jnp.int32, (LANES, LANES), 0)
    c_i = lax.broadcasted_iota(jnp.int32, (LANES, LANES), 1)
    tri = jnp.where(r_i >= c_i, 1.0, 0.0).astype(BF16)

    def cum_block(x, carry):
        acc = carry
        for p in _split_bf16(x, T):
            acc = acc + _mm(tri, p)
        return acc

    xm = jnp.concatenate([lf_ref[n_real:n_real + N_META, :], jnp.zeros((LANES - N_META, LANES), F32)], axis=0)
    cm = cum_block(xm, jnp.zeros((1, LANES), F32))
    c_ref[n_real:n_real + N_META, :] = cm[:N_META, :]
    base = cm[N_META - 1:N_META, :]

    def body(t, carries):
        out = []
        for b, carry in enumerate(carries):
            r0 = pl.multiple_of(b * seq + t * LANES, LANES)
            c = cum_block(lf_ref[pl.ds(r0, LANES), :], carry)
            c_ref[pl.ds(r0, LANES), :] = c
            out.append(c[LANES - 1:LANES, :])
        return tuple(out)
    lax.fori_loop(0, seq // LANES, body, (base,) * n_batch)

    k_i = lax.broadcasted_iota(jnp.int32, (T * LANES, LANES), 0)
    n_i = lax.broadcasted_iota(jnp.int32, (T * LANES, LANES), 1)
    sel_q = jnp.zeros((T * LANES, LANES), F32)
    sel_k = jnp.zeros((T * LANES, LANES), F32)
    lane = lax.broadcasted_iota(jnp.int32, (1, LANES), 1)
    ones_q = jnp.zeros((1, LANES), F32)
    ones_k = jnp.zeros((1, LANES), F32)
    for h in range(N_HEADS):
        for p in range(T):
            row = p * LANES + h
            sel_q = jnp.where((k_i == row) & (n_i == 2 * T * h + p), 1.0, sel_q)
            sel_k = jnp.where((k_i == row) & (n_i == 2 * T * h + T + p), -1.0, sel_k)
            ones_q = jnp.where(lane == 2 * T * h + T + p, 1.0, ones_q)
            ones_k = jnp.where(lane == 2 * T * h + p, 1.0, ones_k)
    sel_q = sel_q.astype(BF16)
    sel_k = sel_k.astype(BF16)

    def expand(c):
        pieces = jnp.concatenate(_split_bf16(c, T), axis=1)
        return ((_mm(pieces, sel_q) + ones_q).astype(BF16), (_mm(pieces, sel_k) + ones_k).astype(BF16))

    chunk = ROW_TILE_REAL

    def ebody(t, _):
        r0 = pl.multiple_of(t * chunk, chunk)
        q, k = expand(c_ref[pl.ds(r0, chunk), :])
        qc_ref[pl.ds(r0, chunk), :] = q
        kc_ref[pl.ds(r0, chunk), :] = k
        return 0
    lax.fori_loop(0, n_real // chunk, ebody, 0)
    _, km = expand(c_ref[n_real:n_real + N_META, :])
    kc_ref[n_real:n_real + N_META, :] = km


def _decay(lf, *, n_batch, seq):
    n_real = n_batch * seq
    return pl.pallas_call(
        functools.partial(_decay_kernel, n_batch=n_batch, seq=seq),
        out_shape=[jax.ShapeDtypeStruct((n_real, LANES), BF16),
                   jax.ShapeDtypeStruct((n_real + N_META, LANES), BF16)],
        scratch_shapes=[pltpu.VMEM((n_real + N_META, LANES), F32)],
        compiler_params=pltpu.CompilerParams(vmem_limit_bytes=VMEM_LIMIT),
        name="decay",
    )(lf)


def _eye(n):
    return jnp.where(lax.broadcasted_iota(jnp.int32, (n, n), 0) == lax.broadcasted_iota(jnp.int32, (n, n), 1),
                     1.0, 0.0).astype(BF16)


def _stage_keys_values(kpad_ref, vt_ref, meta_k_refs, v_ref, mv_ref, *, seq):
    eye = _eye(HEAD_DIM)
    kpad_ref[...] = jnp.zeros_like(kpad_ref)
    for n, ref in enumerate(meta_k_refs):
        kpad_ref[0:N_META, n * HEAD_DIM:(n + 1) * HEAD_DIM] = ref[...]
    vpad = jnp.concatenate([mv_ref[...], jnp.zeros((LANES - N_META, HEAD_DIM), BF16)], axis=0)
    vt_ref[:, seq:seq + LANES] = _mm_nt(eye, vpad).astype(BF16)
    chunk = 2 * ATT_SUB
    for c0 in range(0, seq, chunk):
        vt_ref[:, c0:c0 + chunk] = _mm_nt(eye, v_ref[c0:c0 + chunk, :]).astype(BF16)


def _store_transposed(o_ref, r0, acc_t):
    o_ref[pl.ds(r0, ATT_SUB), :] = acc_t.T.astype(o_ref.dtype)


def _fox_head(q_ref, qc_ref, k_ref, kc_ref, v_ref, mk_ref, mkc_ref, mv_ref, o_ref, kpad_ref, vt_ref,
              sa_ref, sb_ref, *, seq):
    T = DECAY_TERMS
    h = pl.program_id(1)
    _stage_keys_values(kpad_ref, vt_ref, (mk_ref, mkc_ref), v_ref, mv_ref, seq=seq)

    lane = lax.broadcasted_iota(jnp.int32, (1, LANES), 1)
    head_lanes = (lane >= 2 * T * h) & (lane < 2 * T * (h + 1))
    meta_valid = lax.broadcasted_iota(jnp.int32, (LANES, ATT_SUB), 0) < N_META
    causal = (lax.broadcasted_iota(jnp.int32, (ATT_SUB, ATT_SUB), 0)
              <= lax.broadcasted_iota(jnp.int32, (ATT_SUB, ATT_SUB), 1))
    all_subs = tuple(range(ATT_NSUB))

    def keys(c0):
        return jnp.concatenate([k_ref[pl.ds(c0, ATT_SUB), :], kc_ref[pl.ds(c0, ATT_SUB), :]], axis=1)

    def online(s, vt, m, l, acc):
        m_new = jnp.maximum(m, jnp.max(s, axis=0, keepdims=True))
        alpha = jnp.exp2(m - m_new)
        p = jnp.exp2(s - m_new)
        l = alpha * l + jnp.sum(p, axis=0, keepdims=True)
        acc = alpha * acc + _mm(vt, p.astype(BF16))
        return m_new, l, acc

    def block(i):
        r0 = pl.multiple_of(i * ATT_TQ, ATT_TQ)
        qx = jnp.where(head_lanes, qc_ref[pl.ds(r0, ATT_TQ), :], jnp.zeros((), BF16))
        qa = jnp.concatenate([q_ref[pl.ds(r0, ATT_TQ), :], qx], axis=1)
        subs = [qa[n * ATT_SUB:(n + 1) * ATT_SUB] for n in all_subs]

        def issue(buf_ref, c0, which=all_subs):
            ka = keys(c0)
            for n in which:
                buf_ref[n] = _mm_nt(ka, subs[n])

        def consume(buf_ref, c0, state, masked=None, which=all_subs):
            vt = vt_ref[:, pl.ds(c0, ATT_SUB)]
            out = list(state)
            for n in which:
                s = buf_ref[n]
                if n == masked:
                    s = jnp.where(causal, s, -jnp.inf)
                out[n] = online(s, vt, *out[n])
            return tuple(out)

        issue(sa_ref, 0)

        state = []
        for qs in subs:
            s = jnp.where(meta_valid, _mm_nt(kpad_ref[...], qs), -jnp.inf)
            m = jnp.max(s, axis=0, keepdims=True)
            p = jnp.exp2(s - m)
            state.append((m, jnp.sum(p, axis=0, keepdims=True), _mm(vt_ref[:, seq:seq + LANES], p.astype(BF16))))

        def kv_pair(t, state):
            c0 = pl.multiple_of(t * (2 * ATT_SUB), 2 * ATT_SUB)
            c1 = pl.multiple_of(c0 + ATT_SUB, ATT_SUB)
            issue(sb_ref, c1)
            state = consume(sa_ref, c0, state)
            issue(sa_ref, pl.multiple_of(c0 + 2 * ATT_SUB, 2 * ATT_SUB))
            return consume(sb_ref, c1, state)

        def finish(state):
            bufs = (sa_ref, sb_ref)
            for kb in all_subs:
                if kb + 1 < ATT_NSUB:
                    issue(bufs[(kb + 1) % 2], pl.multiple_of(r0 + (kb + 1) * ATT_SUB, ATT_SUB), all_subs[kb + 1:])
                state = consume(bufs[kb % 2], pl.multiple_of(r0 + kb * ATT_SUB, ATT_SUB), state, kb, all_subs[kb:])
            for n, (m, l, acc) in enumerate(state):
                _store_transposed(o_ref, pl.multiple_of(r0 + n * ATT_SUB, ATT_SUB), acc / l)

        return tuple(state), kv_pair, finish

    return block


def _sb_head(q_ref, k_ref, v_ref, mk_ref, mv_ref, o_ref, kpad_ref, vt_ref, *, seq):
    _stage_keys_values(kpad_ref, vt_ref, (mk_ref,), v_ref, mv_ref, seq=seq)

    def later_keys(n):
        return jnp.where(lax.broadcasted_iota(jnp.int32, (n, n), 0) < lax.broadcasted_iota(jnp.int32, (n, n), 1),
                         1.0, 0.0).astype(BF16)

    u_blk = later_keys(ATT_SUB)
    u_meta = later_keys(LANES)
    meta_valid = lax.broadcasted_iota(jnp.int32, (LANES, ATT_SUB), 0) < N_META
    strict = (lax.broadcasted_iota(jnp.int32, (ATT_SUB, ATT_SUB), 0)
              < lax.broadcasted_iota(jnp.int32, (ATT_SUB, ATT_SUB), 1))
    all_subs = tuple(range(ATT_NSUB))

    def steps(items, states):
        states = list(states)
        zs = [_mm_nt(kblk, qs) for _, qs, kblk, _, _, _ in items]
        own, cums = [], []
        for z, (_, _, _, _, u, mask) in zip(zs, items):
            sp = jnp.maximum(jnp.log2(1.0 + jnp.exp2(jnp.minimum(z, MAX_EXP2))), z)
            if mask is not None:
                sp = jnp.where(mask, sp, 0.0)
            pieces = _split_bf16(sp, SUFFIX_TERMS)
            own.append((z - sp, sp[0:1, :]))
            cums.append(functools.reduce(lambda a, b: a + b, [_mm(u, p) for p in pieces]))
        for (log_beta, sp_first), cum, (n, _, _, vt, _, mask) in zip(own, cums, items):
            r, acc = states[n]
            w = jnp.exp2(log_beta - cum - r)
            if mask is not None:
                w = jnp.where(mask, w, 0.0)
            states[n] = (r + (cum[0:1, :] + sp_first), acc + _mm(vt, w.astype(BF16)))
        return tuple(states)

    def block(i):
        r0 = pl.multiple_of(i * ATT_TQ, ATT_TQ)
        q = [q_ref[pl.ds(pl.multiple_of(r0 + n * ATT_SUB, ATT_SUB), ATT_SUB), :] for n in all_subs]
        zero = (jnp.zeros((1, ATT_SUB), F32), jnp.zeros((HEAD_DIM, ATT_SUB), F32))

        def item(n, c0, mask):
            return (n, q[n], k_ref[pl.ds(c0, ATT_SUB), :], vt_ref[:, pl.ds(c0, ATT_SUB)], u_blk, mask)

        def pair_items(jj):
            items = []
            for d in (1, 2):
                c0 = pl.multiple_of(r0 - (2 * jj + d) * ATT_SUB, ATT_SUB)
                items += [item(n, c0, None) for n in all_subs]
            return items

        diag = []
        for kb in reversed(all_subs):
            c0 = pl.multiple_of(r0 + kb * ATT_SUB, ATT_SUB)
            diag += [item(n, c0, strict if n == kb else None) for n in all_subs[kb:]]
        states = steps(diag, (zero,) * ATT_NSUB)

        def kv_pair(jj, states):
            return steps(pair_items(jj), states)

        def finish(states):
            meta = [(n, q[n], kpad_ref[...], vt_ref[:, seq:seq + LANES], u_meta, meta_valid) for n in all_subs]
            for n, (_, acc) in enumerate(steps(meta, states)):
                _store_transposed(o_ref, pl.multiple_of(r0 + n * ATT_SUB, ATT_SUB), acc)

        return states, kv_pair, finish

    return block


def _attention_kernel(fq, fqc, fk, fkc, fv, fmk, fmkc, fmv, sq, sk, sv, smk, smv, of_ref, os_ref,
                      f_kpad, f_vt, f_sa, f_sb, s_kpad, s_vt, *, seq):
    fox = _fox_head(fq, fqc, fk, fkc, fv, fmk, fmkc, fmv, of_ref, f_kpad, f_vt, f_sa, f_sb, seq=seq)
    stick = _sb_head(sq, sk, sv, smk, smv, os_ref, s_kpad, s_vt, seq=seq)

    def q_block(i, _):
        f_state, f_pair, f_finish = fox(i)
        s_state, s_pair, s_finish = stick(i)

        def kv_pair(t, carry):
            return f_pair(t, carry[0]), s_pair(t, carry[1])

        f_state, s_state = lax.fori_loop(0, i * (ATT_NSUB // 2), kv_pair, (f_state, s_state))
        f_finish(f_state)
        s_finish(s_state)
        return 0

    lax.fori_loop(0, seq // ATT_TQ, q_block, 0)


def _head_spec(rows, col_block0):
    return pl.BlockSpec((rows, HEAD_DIM), lambda b, h: (b, col_block0 + h))


def _meta_spec(n_real, col_block0):
    return pl.BlockSpec((N_META, HEAD_DIM), lambda b, h: (n_real // N_META, col_block0 + h))


def _attention(fox_qkv, qc, kc, sb_qkv, *, n_batch, seq):
    n_real = n_batch * seq
    shared = pl.BlockSpec((seq, LANES), lambda b, h: (b, 0))
    qkv_specs = [_head_spec(seq, 0), _head_spec(seq, N_HEADS), _head_spec(seq, 2 * N_HEADS)]
    meta_specs = [_meta_spec(n_real, N_HEADS), _meta_spec(n_real, 2 * N_HEADS)]
    score_buf = pltpu.VMEM((ATT_NSUB, ATT_SUB, ATT_SUB), F32)
    return pl.pallas_call(
        functools.partial(_attention_kernel, seq=seq),
        grid=(n_batch, N_HEADS),
        in_specs=[qkv_specs[0], shared, qkv_specs[1], shared, qkv_specs[2],
                  meta_specs[0], pl.BlockSpec((N_META, LANES), lambda b, h: (n_real // N_META, 0)), meta_specs[1],
                  *qkv_specs, *meta_specs],
        out_specs=[_head_spec(seq, 0), _head_spec(seq, 0)],
        out_shape=[jax.ShapeDtypeStruct((n_real, WIDTH), BF16)] * 2,
        scratch_shapes=[pltpu.VMEM((LANES, 2 * HEAD_DIM), BF16), pltpu.VMEM((HEAD_DIM, seq + LANES), BF16),
                        score_buf, score_buf,
                        pltpu.VMEM((LANES, HEAD_DIM), BF16), pltpu.VMEM((HEAD_DIM, seq + LANES), BF16)],
        compiler_params=_params(2),
        name="attention",
    )(fox_qkv, qc, fox_qkv, kc, fox_qkv, fox_qkv, kc, fox_qkv, sb_qkv, sb_qkv, sb_qkv, sb_qkv, sb_qkv)


def _mix_merge_kernel(of_ref, os_ref, gf_ref, gs_ref, wbf_ref, wbs_ref, o_ref, wbf_b, wbs_b):
    @pl.when(pl.program_id(1) == 0)
    def _():
        wbf_b[...] = wbf_ref[...].astype(BF16)
        wbs_b[...] = wbs_ref[...].astype(BF16)

    for rows in _row_chunks(of_ref.shape[0]):
        tf = _mm(of_ref[rows, :], wbf_b[...])
        ts = _mm(os_ref[rows, :], wbs_b[...])
        merged = gf_ref[rows, :].astype(F32) * tf + gs_ref[rows, :].astype(F32) * ts
        o_ref[rows, :] = merged.astype(o_ref.dtype)


def _mix_merge(o_fox, o_sb, gates, wbf, wbs, *, rows, tm, tn):
    ni, nj = rows // tm, D_MODEL // tn
    return pl.pallas_call(
        _mix_merge_kernel,
        grid=(nj, ni),
        in_specs=[
            pl.BlockSpec((tm, WIDTH), lambda j, i: (i, 0)),
            pl.BlockSpec((tm, WIDTH), lambda j, i: (i, 0)),
            pl.BlockSpec((tm, tn), lambda j, i: (i, j)),
            pl.BlockSpec((tm, tn), lambda j, i: (i, nj + j)),
            pl.BlockSpec((WIDTH, tn), lambda j, i: (0, j)),
            pl.BlockSpec((WIDTH, tn), lambda j, i: (0, j)),
        ],
        out_specs=pl.BlockSpec((tm, tn), lambda j, i: (i, j)),
        out_shape=jax.ShapeDtypeStruct((rows, D_MODEL), BF16),
        scratch_shapes=[pltpu.VMEM((WIDTH, tn), BF16), pltpu.VMEM((WIDTH, tn), BF16)],
        compiler_params=_params(2),
        name="mix_merge",
    )(o_fox, o_sb, gates, gates, wbf, wbs)


def _mix_proj_kernel(m_ref, w_ref, h_ref, o_ref, wb_ref):
    @pl.when(pl.program_id(1) == 0)
    def _():
        wb_ref[...] = w_ref[...].astype(BF16)

    for rows in _row_chunks(m_ref.shape[0]):
        o_ref[rows, :] = h_ref[rows, :] + _mm(m_ref[rows, :], wb_ref[...])


def _mix_proj(merged, wo, h, *, rows, tm, tn):
    ni, nj = rows // tm, D_MODEL // tn
    return pl.pallas_call(
        _mix_proj_kernel,
        grid=(nj, ni),
        in_specs=[
            pl.BlockSpec((tm, D_MODEL), lambda j, i: (i, 0)),
            pl.BlockSpec((D_MODEL, tn), lambda j, i: (0, j)),
            pl.BlockSpec((tm, tn), lambda j, i: (i, j)),
        ],
        out_specs=pl.BlockSpec((tm, tn), lambda j, i: (i, j)),
        out_shape=jax.ShapeDtypeStruct((rows, D_MODEL), F32),
        scratch_shapes=[pltpu.VMEM((D_MODEL, tn), BF16)],
        compiler_params=_params(2),
        name="mix_proj",
    )(merged, wo, h)


def kernel(x, meta_tokens, ffn1_norm, ffn1_w_gate, ffn1_w_up, ffn1_w_down, mix_norm, w_in, b_forget, fox_q_norm, fox_k_norm, w_branch_fox, w_branch_sb, w_out, ffn2_norm, ffn2_w_gate, ffn2_w_up, ffn2_w_down):
    n_batch, seq, _ = x.shape
    n_real = n_batch * seq
    n_all = n_real + N_META
    assert ffn1_norm.shape[0] == 1
    assert n_all % ROW_TILE_ALL == 0 and n_real % ROW_TILE_REAL == 0 and seq % ATT_TQ == 0

    wt = w_in[0].T
    r_f = 3 * WIDTH
    r_sb = r_f + N_HEADS
    r_gate = r_sb + 3 * WIDTH
    w_f = jnp.pad(w_in[0][:, r_f:r_sb], ((0, 0), (0, LANES - N_HEADS)))
    b_f = jnp.pad(b_forget[0].reshape(1, N_HEADS), ((0, 0), (0, LANES - N_HEADS)))
    fox_gain = jnp.concatenate([fox_q_norm[0].reshape(1, WIDTH) * (SCALE * LOG2E), fox_k_norm[0].reshape(1, WIDTH),
                                jnp.ones((1, WIDTH), F32)], axis=1)
    sb_scale = jnp.concatenate([jnp.full((1, WIDTH), SCALE * LOG2E, F32), jnp.ones((1, 2 * WIDTH), F32)], axis=1)

    h1, n1, log_f = _ffn(x.reshape(n_real, D_MODEL), meta_tokens.astype(F32), ffn1_norm[0], ffn1_w_gate[0],
                         ffn1_w_up[0], ffn1_w_down[0], rows=n_all, tm=ROW_TILE_ALL, next_gain=mix_norm[0],
                         forget=(w_f, b_f))

    proj = functools.partial(_proj, n1, rows=n_all, tm=ROW_TILE_ALL)
    fox_qkv = proj(wt, 0, 3 * WIDTH, fox_gain, tn=PROJ_TILE_N, epilogue=_fox_epilogue,
                   out_dtype=BF16, name="proj_fox")
    sb_qkv = proj(wt, r_sb, 3 * WIDTH, sb_scale, tn=PROJ_TILE_N, epilogue=_scale_epilogue,
                  out_dtype=BF16, name="proj_sb")
    gates = _proj(n1, wt, r_gate, 2 * D_MODEL, None, rows=n_real, tm=ROW_TILE_REAL,
                  tn=PROJ_TILE_N, epilogue=_gate_epilogue, out_dtype=BF16, name="proj_gates")

    qc, kc = _decay(log_f, n_batch=n_batch, seq=seq)
    o_fox, o_sb = _attention(fox_qkv, qc, kc, sb_qkv, n_batch=n_batch, seq=seq)

    merged = _mix_merge(o_fox, o_sb, gates, w_branch_fox[0], w_branch_sb[0],
                        rows=n_real, tm=ROW_TILE_REAL, tn=PROJ_TILE_N)
    h2 = _mix_proj(merged, w_out[0], h1, rows=n_real, tm=ROW_TILE_REAL, tn=PROJ_TILE_N)
    h3 = _ffn(h2, None, ffn2_norm[0], ffn2_w_gate[0], ffn2_w_up[0], ffn2_w_down[0],
              rows=n_real, tm=ROW_TILE_REAL)
    return h3.reshape(n_batch, seq, D_MODEL)
```
